```python
import math
import jax, jax.numpy as jnp
from jax import lax
import numpy as np

D_MODEL = 1024
BATCH = 16
SEQ = 4096
DEPTH = 1
DEC_BATCH = 32
DEC_SEQ = 2048
PAST_LEN = 128

GRID_W = 64
EPS = 1e-6
POOL_WIDTH = D_MODEL // 2
POOL_WINDOWS = (2, 4, 8, 16)
POOL_GROUPS = len(POOL_WINDOWS)
POOL_GROUP_DIM = POOL_WIDTH // POOL_GROUPS
HEAD_DIM = 64
N_Q_HEADS = 8
N_KV_HEADS = 2
GQA_GROUP = N_Q_HEADS // N_KV_HEADS
ATTN_WIDTH = N_Q_HEADS * HEAD_DIM
KV_WIDTH = N_KV_HEADS * HEAD_DIM
MIX_WIDTH = POOL_WIDTH + ATTN_WIDTH
IN_WIDTH = POOL_WIDTH + ATTN_WIDTH + 2 * KV_WIDTH
ROPE_HALF = HEAD_DIM // 2
ROPE_THETA = 10000.0
ATTN_SCALE = 1.0 / math.sqrt(HEAD_DIM)
Q_BLOCK = 128
PEER_HEADS = 8
PEER_NKEYS = 128
PEER_EXPERTS = PEER_NKEYS * PEER_NKEYS
PEER_KEY_DIM = 256
PEER_HALF = PEER_KEY_DIM // 2
PEER_TOPK = 16
PEER_CHUNK = 128

kernel_name = "hymba_pool_gqa_peer_encoder"


def rms_norm(x, g):
    xf = x.astype(jnp.float32)
    y = xf * lax.rsqrt(jnp.mean(xf * xf, axis=-1, keepdims=True) + EPS)
    return (y * g.astype(jnp.float32)).astype(x.dtype)


def axial_rope_angles(n):
    rows = n // GRID_W
    row = jnp.repeat(jnp.arange(rows, dtype=jnp.float32), GRID_W)
    col = jnp.tile(jnp.arange(GRID_W, dtype=jnp.float32), rows)
    inv = 1.0 / (ROPE_THETA ** (jnp.arange(0, ROPE_HALF, 2, dtype=jnp.float32) / ROPE_HALF))
    return row[:, None] * inv, col[:, None] * inv


def rotate(xh, ang):
    a, b = jnp.split(xh, 2, axis=-1)
    c = jnp.cos(ang)[None, :, None, :]
    s = jnp.sin(ang)[None, :, None, :]
    return jnp.concatenate([a * c - b * s, a * s + b * c], axis=-1)


def apply_axial_rope(x, ang_r, ang_c):
    xf = x.astype(jnp.float32)
    return jnp.concatenate([rotate(xf[..., :ROPE_HALF], ang_r),
                            rotate(xf[..., ROPE_HALF:], ang_c)], axis=-1)


def pool_mixer(xp, pool_w, pool_scale):
    b, n, _ = xp.shape
    xf = xp.astype(jnp.float32)
    cs = jnp.concatenate([jnp.zeros((b, 1, POOL_WIDTH), jnp.float32),
                          jnp.cumsum(xf, axis=1)], axis=1)
    t = jnp.arange(n)
    outs = []
    for gi, w in enumerate(POOL_WINDOWS):
        lo = jnp.clip(t - w // 2, 0, n)
        hi = jnp.clip(t + (w - w // 2), 0, n)
        sl = slice(gi * POOL_GROUP_DIM, (gi + 1) * POOL_GROUP_DIM)
        csg = cs[..., sl]
        mean = (csg[:, hi] - csg[:, lo]) / (hi - lo).astype(jnp.float32)[None, :, None]
        outs.append((mean - xf[..., sl]) @ pool_w[gi].astype(jnp.float32))
    return (jnp.concatenate(outs, axis=-1) * pool_scale.astype(jnp.float32)).astype(xp.dtype)


def gqa_attention(zq, zk, zv, q_norm_g, k_norm_g):
    b, n, _ = zq.shape
    q = zq.reshape(b, n, N_Q_HEADS, HEAD_DIM)
    k = zk.reshape(b, n, N_KV_HEADS, HEAD_DIM)
    v = zv.reshape(b, n, N_KV_HEADS, HEAD_DIM).astype(jnp.float32)
    ang_r, ang_c = axial_rope_angles(n)
    q = apply_axial_rope(rms_norm(q, q_norm_g), ang_r, ang_c) * ATTN_SCALE
    k = apply_axial_rope(rms_norm(k, k_norm_g), ang_r, ang_c)
    nb = n // Q_BLOCK
    qb = q.reshape(b, nb, Q_BLOCK, N_KV_HEADS, GQA_GROUP, HEAD_DIM).transpose(1, 0, 2, 3, 4, 5)

    def block_fn(qblk):
        s = jnp.einsum('bqkgd,bskd->bkgqs', qblk, k)
        p = jax.nn.softmax(s, axis=-1)
        return jnp.einsum('bkgqs,bskd->bqkgd', p, v)

    o = lax.map(block_fn, qb)
    return o.transpose(1, 0, 2, 3, 4, 5).reshape(b, n, ATTN_WIDTH).astype(zq.dtype)


def peer_ffn(xn, wq, subkeys, u, v):
    b, n, d = xn.shape
    xt = xn.reshape(-1, PEER_CHUNK, d)
    sk = subkeys.astype(jnp.float32)

    def chunk_fn(xc):
        q = (xc @ wq).astype(jnp.float32).reshape(PEER_CHUNK, PEER_HEADS, 2, PEER_HALF)
        s = jnp.einsum('chpe,hpke->chpk', q, sk)
        sv, si = lax.top_k(s, PEER_TOPK)
        comb = (sv[:, :, 0, :, None] + sv[:, :, 1, None, :]).reshape(PEER_CHUNK, PEER_HEADS, PEER_TOPK * PEER_TOPK)
        cv, ci = lax.top_k(comb, PEER_TOPK)
        i1 = jnp.take_along_axis(si[:, :, 0], ci // PEER_TOPK, axis=-1)
        i2 = jnp.take_along_axis(si[:, :, 1], ci % PEER_TOPK, axis=-1)
        e = i1 * PEER_NKEYS + i2
        g = jax.nn.softmax(cv, axis=-1)
        hid = jnp.einsum('cd,chkd->chk', xc, u[e], preferred_element_type=jnp.float32)
        a = jax.nn.gelu(hid, approximate=False) * g
        out = jnp.einsum('chk,chkd->cd', a, v[e].astype(jnp.float32))
        return out.astype(xc.dtype)

    return lax.map(chunk_fn, xt).reshape(b, n, d)


def encoder_layer(x, norm1_g, w_in, pool_w, pool_scale, q_norm_g, k_norm_g, w_out,
                  norm2_g, peer_wq, peer_subkeys, peer_u, peer_v):
    h = rms_norm(x, norm1_g)
    z = h @ w_in
    zp = z[..., :POOL_WIDTH]
    zq = z[..., POOL_WIDTH:POOL_WIDTH + ATTN_WIDTH]
    zk = z[..., POOL_WIDTH + ATTN_WIDTH:POOL_WIDTH + ATTN_WIDTH + KV_WIDTH]
    zv = z[..., POOL_WIDTH + ATTN_WIDTH + KV_WIDTH:]
    mix = jnp.concatenate([pool_mixer(zp, pool_w, pool_scale),
                           gqa_attention(zq, zk, zv, q_norm_g, k_norm_g)], axis=-1)
    x = x + mix @ w_out
    x = x + peer_ffn(rms_norm(x, norm2_g), peer_wq, peer_subkeys, peer_u, peer_v)
    return x


def setup_inputs(seed: int = 0) -> dict:
    key = jax.random.key(seed)
    ks = jax.random.split(key, 15)
    f32 = jnp.float32
    nrm = lambda k, shape, s: jax.random.normal(k, shape, f32) * s
    return {
        "x_prompt": nrm(ks[0], (BATCH, SEQ, D_MODEL), 1.0),
        "x_sample": nrm(ks[1], (DEC_BATCH, DEC_SEQ, D_MODEL), 1.0),
        "norm1_g": 1.0 + nrm(ks[2], (DEPTH, D_MODEL), 0.02),
        "w_in": nrm(ks[3], (DEPTH, D_MODEL, IN_WIDTH), D_MODEL ** -0.5),
        "pool_w": nrm(ks[4], (DEPTH, POOL_GROUPS, POOL_GROUP_DIM, POOL_GROUP_DIM), POOL_GROUP_DIM ** -0.5),
        "pool_scale": 1.0 + nrm(ks[5], (DEPTH, POOL_WIDTH), 0.02),
        "q_norm_g": 1.0 + nrm(ks[6], (DEPTH, HEAD_DIM), 0.02),
        "k_norm_g": 1.0 + nrm(ks[7], (DEPTH, HEAD_DIM), 0.02),
        "w_out": nrm(ks[8], (DEPTH, MIX_WIDTH, D_MODEL), MIX_WIDTH ** -0.5),
        "norm2_g": 1.0 + nrm(ks[9], (DEPTH, D_MODEL), 0.02),
        "peer_wq": nrm(ks[10], (DEPTH, D_MODEL, PEER_HEADS * PEER_KEY_DIM), D_MODEL ** -0.5),
        "peer_subkeys": nrm(ks[11], (DEPTH, PEER_HEADS, 2, PEER_NKEYS, PEER_HALF), PEER_HALF ** -0.5),
        "peer_u": nrm(ks[12], (DEPTH, PEER_EXPERTS, D_MODEL), D_MODEL ** -0.5),
        "peer_v": nrm(ks[13], (DEPTH, PEER_EXPERTS, D_MODEL), PEER_HEADS ** -0.5),
    }


def reference(x_prompt, x_sample, norm1_g, w_in, pool_w, pool_scale, q_norm_g, k_norm_g,
              w_out, norm2_g, peer_wq, peer_subkeys, peer_u, peer_v):
    y_prompt = x_prompt
    y_sample = x_sample
    for l in range(DEPTH):
        params = (norm1_g[l], w_in[l], pool_w[l], pool_scale[l], q_norm_g[l], k_norm_g[l],
                  w_out[l], norm2_g[l], peer_wq[l], peer_subkeys[l], peer_u[l], peer_v[l])
        y_prompt = encoder_layer(y_prompt, *params)
        y_sample = encoder_layer(y_sample, *params)
    return (y_prompt, y_sample)
```

```python
import functools
import math

import numpy as np
import jax
import jax.numpy as jnp
from jax import lax
from jax.experimental import pallas as pl
from jax.experimental.pallas import tpu as pltpu

F32 = jnp.float32
BF16 = jnp.bfloat16

EPS = 1e-6
GRID_W = 64
POOL_WINDOWS = (2, 4, 8, 16)
POOL_GROUP_DIM = 128
POOL_WIDTH = 512
POOL_HALO = 8
HEAD_DIM = 64
N_Q_HEADS = 8
N_KV_HEADS = 2
GQA_GROUP = N_Q_HEADS // N_KV_HEADS
ATTN_WIDTH = N_Q_HEADS * HEAD_DIM
KV_WIDTH = N_KV_HEADS * HEAD_DIM
ROPE_HALF = HEAD_DIM // 2
ROPE_QUARTER = ROPE_HALF // 2
ROPE_THETA = 10000.0
ATTN_SCALE = 1.0 / math.sqrt(HEAD_DIM)
PEER_HEADS = 8
PEER_NKEYS = 128
PEER_HALF = 128
PEER_TOPK = 16
PEER_PICKS = PEER_HEADS * PEER_TOPK

VMEM_LIMIT = 48 * 1024 * 1024


def _tile(n, want):
    t = min(n, want)
    assert n % t == 0, (n, t)
    return t


def _head_mean_square(a, mavg):
    sq = a * a
    hi = sq.astype(BF16)
    lo = (sq - hi.astype(F32)).astype(BF16)
    return (jnp.dot(hi, mavg, preferred_element_type=F32)
            + jnp.dot(lo, mavg, preferred_element_type=F32))


def _in_proj_kernel(x_ref, g1_ref, w_ref, pw_ref, gq_ref, gqs_ref, gk_ref, gks_ref,
                    cos_ref, sin_ref, mavg_ref, yp_ref, q_ref, kt_ref, v_ref):
    x = x_ref[...]
    ms = jnp.mean(x * x, axis=-1, keepdims=True)
    h = (x * lax.rsqrt(ms + EPS) * g1_ref[...]).astype(BF16)
    z = jnp.dot(h, w_ref[...], preferred_element_type=F32)
    for g in range(len(POOL_WINDOWS)):
        sl = slice(g * POOL_GROUP_DIM, (g + 1) * POOL_GROUP_DIM)
        yp_ref[:, sl] = jnp.dot(z[:, sl].astype(BF16), pw_ref[g], preferred_element_type=F32)
    o_q = POOL_WIDTH
    o_k = o_q + ATTN_WIDTH
    o_v = o_k + KV_WIDTH
    o_qs = o_v + KV_WIDTH
    o_ks = o_qs + ATTN_WIDTH
    zq, zk, zv = z[:, o_q:o_k], z[:, o_k:o_v], z[:, o_v:o_qs]
    zqs, zks = z[:, o_qs:o_ks], z[:, o_ks:o_ks + KV_WIDTH]
    cos = cos_ref[...]
    sin = sin_ref[...]
    rq = lax.rsqrt(_head_mean_square(zq, mavg_ref[...]) + EPS)
    q = ((zq * rq * gq_ref[...]) * cos + (zqs * rq * gqs_ref[...]) * sin) * ATTN_SCALE
    q_ref[...] = q.astype(BF16)
    rk = lax.rsqrt(_head_mean_square(zk, mavg_ref[:KV_WIDTH, :KV_WIDTH]) + EPS)
    k = (zk * rk * gk_ref[...]) * cos[:, :KV_WIDTH] + (zks * rk * gks_ref[...]) * sin[:, :KV_WIDTH]
    kt_ref[0] = k.T.astype(BF16)
    v_ref[...] = zv.astype(BF16)


def _in_proj(x2d, n, g1, w_ext, pw, gq, gqs, gk, gks, cos_t, sin_t, mavg):
    N, D = x2d.shape
    T = _tile(n, 512)
    tiles_per_seq = n // T
    const = lambda *s: pl.BlockSpec(s, lambda i: (0,) * len(s))
    return pl.pallas_call(
        _in_proj_kernel,
        grid=(N // T,),
        in_specs=[
            pl.BlockSpec((T, D), lambda i: (i, 0)),
            const(1, D), const(*w_ext.shape), const(*pw.shape),
            const(1, ATTN_WIDTH), const(1, ATTN_WIDTH), const(1, KV_WIDTH), const(1, KV_WIDTH),
            pl.BlockSpec((T, ATTN_WIDTH), lambda i: (i % tiles_per_seq, 0)),
            pl.BlockSpec((T, ATTN_WIDTH), lambda i: (i % tiles_per_seq, 0)),
            const(ATTN_WIDTH, ATTN_WIDTH),
        ],
        out_specs=[
            pl.BlockSpec((T, POOL_WIDTH), lambda i: (i, 0)),
            pl.BlockSpec((T, ATTN_WIDTH), lambda i: (i, 0)),
            pl.BlockSpec((1, KV_WIDTH, T), lambda i: (i // tiles_per_seq, 0, i % tiles_per_seq)),
            pl.BlockSpec((T, KV_WIDTH), lambda i: (i, 0)),
        ],
        out_shape=[
            jax.ShapeDtypeStruct((N, POOL_WIDTH), F32),
            jax.ShapeDtypeStruct((N, ATTN_WIDTH), BF16),
            jax.ShapeDtypeStruct((N // n, KV_WIDTH, n), BF16),
            jax.ShapeDtypeStruct((N, KV_WIDTH), BF16),
        ],
        compiler_params=pltpu.CompilerParams(
            dimension_semantics=("arbitrary",), vmem_limit_bytes=VMEM_LIMIT),
        name="in_proj",
    )(x2d, g1, w_ext, pw, gq, gqs, gk, gks, cos_t, sin_t, mavg)


def _attn_kernel(q_ref, kt_ref, v_ref, o_ref):
    outs = []
    for j in range(N_KV_HEADS):
        kt = kt_ref[0, j * HEAD_DIM:(j + 1) * HEAD_DIM, :]
        v = v_ref[0, :, j * HEAD_DIM:(j + 1) * HEAD_DIM]
        for g in range(GQA_GROUP):
            hq = j * GQA_GROUP + g
            q = q_ref[0, :, hq * HEAD_DIM:(hq + 1) * HEAD_DIM]
            s = jnp.dot(q, kt, preferred_element_type=F32)
            m = jnp.max(s, axis=-1, keepdims=True)
            p = jnp.exp(s - m)
            l = jnp.sum(p, axis=-1, keepdims=True)
            o = jnp.dot(p.astype(BF16), v, preferred_element_type=F32)
            outs.append(o / l)
    o_ref[0] = jnp.concatenate(outs, axis=-1).astype(BF16)


def _attention(q, kt, v):
    B, n, _ = q.shape
    tq = _tile(n, 256)
    return pl.pallas_call(
        _attn_kernel,
        grid=(B, n // tq),
        in_specs=[
            pl.BlockSpec((1, tq, ATTN_WIDTH), lambda b, i: (b, i, 0)),
            pl.BlockSpec((1, KV_WIDTH, n), lambda b, i: (b, 0, 0)),
            pl.BlockSpec((1, n, KV_WIDTH), lambda b, i: (b, 0, 0)),
        ],
        out_specs=pl.BlockSpec((1, tq, ATTN_WIDTH), lambda b, i: (b, i, 0)),
        out_shape=jax.ShapeDtypeStruct((B, n, ATTN_WIDTH), BF16),
        compiler_params=pltpu.CompilerParams(
            dimension_semantics=("arbitrary", "arbitrary"), vmem_limit_bytes=VMEM_LIMIT),
        name="attention",
    )(q, kt, v)


def _out_proj_kernel(n, x_ref, yp_ref, prev_ref, next_ref, at_ref, sc_ref, w_ref, g2_ref,
                     x1_ref, xn_ref):
    T = x_ref.shape[0]
    tiles_per_seq = n // T
    si = pl.program_id(0) % tiles_per_seq
    not_first = (si > 0).astype(F32)
    not_last = (si < tiles_per_seq - 1).astype(F32)
    cur = yp_ref[...]
    ext = jnp.concatenate([prev_ref[...] * not_first, cur, next_ref[...] * not_last], axis=0)
    t = si * T + lax.broadcasted_iota(jnp.int32, (T, 1), 0)
    pooled = []
    for g, w in enumerate(POOL_WINDOWS):
        sl = slice(g * POOL_GROUP_DIM, (g + 1) * POOL_GROUP_DIM)
        eg = ext[:, sl]
        acc = eg[POOL_HALO - w // 2:POOL_HALO - w // 2 + T]
        for d in range(-w // 2 + 1, w - w // 2):
            acc = acc + eg[POOL_HALO + d:POOL_HALO + d + T]
        cnt = jnp.minimum(t + (w - w // 2), n) - jnp.maximum(t - w // 2, 0)
        pooled.append(acc / cnt.astype(F32) - cur[:, sl])
    pool = jnp.concatenate(pooled, axis=-1) * sc_ref[...]
    mix = jnp.concatenate([pool.astype(BF16), at_ref[...]], axis=-1)
    x1 = x_ref[...] + jnp.dot(mix, w_ref[...], preferred_element_type=F32)
    x1_ref[...] = x1
    ms = jnp.mean(x1 * x1, axis=-1, keepdims=True)
    xn_ref[...] = (x1 * lax.rsqrt(ms + EPS) * g2_ref[...]).astype(BF16)


def _out_proj(x2d, n, yp, attn, scale, w_out, g2):
    N, D = x2d.shape
    T = _tile(n, 512)
    hb = T // POOL_HALO
    last_hb = N // POOL_HALO - 1
    const = lambda *s: pl.BlockSpec(s, lambda i: (0,) * len(s))
    return pl.pallas_call(
        functools.partial(_out_proj_kernel, n),
        grid=(N // T,),
        in_specs=[
            pl.BlockSpec((T, D), lambda i: (i, 0)),
            pl.BlockSpec((T, POOL_WIDTH), lambda i: (i, 0)),
            pl.BlockSpec((POOL_HALO, POOL_WIDTH), lambda i: (jnp.maximum(i * hb - 1, 0), 0)),
            pl.BlockSpec((POOL_HALO, POOL_WIDTH), lambda i: (jnp.minimum((i + 1) * hb, last_hb), 0)),
            pl.BlockSpec((T, ATTN_WIDTH), lambda i: (i, 0)),
            const(1, POOL_WIDTH), const(*w_out.shape), const(1, D),
        ],
        out_specs=[
            pl.BlockSpec((T, D), lambda i: (i, 0)),
            pl.BlockSpec((T, D), lambda i: (i, 0)),
        ],
        out_shape=[
            jax.ShapeDtypeStruct((N, D), F32),
            jax.ShapeDtypeStruct((N, D), BF16),
        ],
        compiler_params=pltpu.CompilerParams(
            dimension_semantics=("arbitrary",), vmem_limit_bytes=VMEM_LIMIT),
        name="out_proj",
    )(x2d, yp, yp, yp, attn, scale, w_out, g2)


def _extract_top(vals, payload, k):
    R = vals.shape[0]
    rows = lax.broadcasted_iota(jnp.int32, vals.shape, 0)
    top_v, top_p = [], []
    for _ in range(k):
        m = jnp.max(vals, axis=0, keepdims=True)
        pos = jnp.min(jnp.where(vals == m, rows, R), axis=0, keepdims=True)
        sel = rows == pos
        top_v.append(m)
        top_p.append(jnp.max(jnp.where(sel, payload, -1), axis=0, keepdims=True))
        vals = jnp.where(sel, -jnp.inf, vals)
    return jnp.concatenate(top_v, axis=0), jnp.concatenate(top_p, axis=0)


def _peer_route_kernel(xn_ref, wqt_ref, sk_ref, e_ref, gt_ref, qt_scr, et_scr):
    T = xn_ref.shape[0]
    qt_scr[...] = lax.dot_general(wqt_ref[...], xn_ref[...], (((1,), (1,)), ((), ())),
                                  preferred_element_type=F32).astype(BF16)
    key_ids = lax.broadcasted_iota(jnp.int32, (PEER_NKEYS, T), 0)

    def head_body(h, carry):
        sub_v, sub_i = [], []
        for p in range(2):
            hp = h * 2 + p
            qhp = qt_scr[pl.ds(pl.multiple_of(hp * PEER_HALF, PEER_HALF), PEER_HALF), :]
            s = jnp.dot(sk_ref[hp], qhp, preferred_element_type=F32)
            tv, ti = _extract_top(s, key_ids, PEER_TOPK)
            sub_v.append(tv)
            sub_i.append(ti)
        comb = jnp.concatenate([sub_v[0][a:a + 1, :] + sub_v[1] for a in range(PEER_TOPK)], axis=0)
        eid = jnp.concatenate([sub_i[0][a:a + 1, :] * PEER_NKEYS + sub_i[1]
                               for a in range(PEER_TOPK)], axis=0)
        cv, ce = _extract_top(comb, eid, PEER_TOPK)
        ex = jnp.exp(cv - cv[0:1, :])
        gate = ex / jnp.sum(ex, axis=0, keepdims=True)
        row0 = pl.multiple_of(h * PEER_TOPK, PEER_TOPK)
        et_scr[pl.ds(row0, PEER_TOPK), :] = ce.astype(F32)
        gt_ref[0, pl.ds(row0, PEER_TOPK), :] = gate
        return carry

    lax.fori_loop(0, PEER_HEADS, head_body, 0)
    e_ref[...] = et_scr[...].T.astype(jnp.int32)


PEER_ROUTE_TILE = 128


def _peer_route(xn, wqt, sk):
    N, D = xn.shape
    T = PEER_ROUTE_TILE
    const = lambda *s: pl.BlockSpec(s, lambda i: (0,) * len(s))
    return pl.pallas_call(
        _peer_route_kernel,
        grid=(N // T,),
        in_specs=[pl.BlockSpec((T, D), lambda i: (i, 0)), const(*wqt.shape), const(*sk.shape)],
        out_specs=[
            pl.BlockSpec((T, PEER_PICKS), lambda i: (i, 0)),
            pl.BlockSpec((1, PEER_PICKS, T), lambda i: (i, 0, 0)),
        ],
        out_shape=[
            jax.ShapeDtypeStruct((N, PEER_PICKS), jnp.int32),
            jax.ShapeDtypeStruct((N // T, PEER_PICKS, T), F32),
        ],
        scratch_shapes=[
            pltpu.VMEM((wqt.shape[0], T), BF16),
            pltpu.VMEM((PEER_PICKS, T), F32),
        ],
        compiler_params=pltpu.CompilerParams(
            dimension_semantics=("arbitrary",), vmem_limit_bytes=VMEM_LIMIT),
        name="peer_route",
    )(xn, wqt, sk)


PEER_TOK_TILE = 8
PEER_ISSUE_UNROLL = 8


def _peer_ffn_kernel(ids_cur_ref, ids_nxt_ref, xn_ref, gt_ref, x1_ref, tab_ref, y_ref, buf, sem):
    TT = xn_ref.shape[0]
    rows = TT * PEER_PICKS
    i = pl.program_id(0)
    nsteps = pl.num_programs(0)

    def row_copy(ids_ref, slot, j):
        return pltpu.make_async_copy(tab_ref.at[pl.ds(ids_ref[0, 0, j], 1)],
                                     buf.at[slot, pl.ds(j, 1)], sem.at[slot])

    def issue(ids_ref, slot):
        def body(c, carry):
            for r in range(PEER_ISSUE_UNROLL):
                row_copy(ids_ref, slot, c * PEER_ISSUE_UNROLL + r).start()
            return carry
        lax.fori_loop(0, rows // PEER_ISSUE_UNROLL, body, 0)

    @pl.when(i == 0)
    def _():
        issue(ids_cur_ref, 0)

    @pl.when(i + 1 < nsteps)
    def _():
        issue(ids_nxt_ref, (i + 1) % 2)

    slot = i % 2
    pltpu.make_async_copy(buf.at[slot], buf.at[slot], sem.at[slot]).wait()

    lanes = lax.broadcasted_iota(jnp.int32, (PEER_PICKS, PEER_ROUTE_TILE), 1) - (i * TT) % PEER_ROUTE_TILE
    hid = jnp.zeros((PEER_PICKS, PEER_ROUTE_TILE), F32)
    for t in range(TT):
        w = buf[slot, t * PEER_PICKS:(t + 1) * PEER_PICKS, :]
        gu = lax.bitcast_convert_type(w & jnp.int32(-65536), F32)
        xt = xn_ref[t:t + 1, :].astype(F32)
        h_t = jnp.sum(gu * xt, axis=-1, keepdims=True)
        hid = jnp.where(lanes == t, h_t, hid)
    act = 0.5 * hid * (1.0 + lax.erf(hid * (1.0 / math.sqrt(2.0)))) * gt_ref[0]
    for t in range(TT):
        w = buf[slot, t * PEER_PICKS:(t + 1) * PEER_PICKS, :]
        gv = lax.bitcast_convert_type(w << 16, F32)
        a_t = jnp.sum(jnp.where(lanes == t, act, 0.0), axis=-1, keepdims=True)
        o_t = jnp.sum(gv * a_t, axis=0, keepdims=True)
        y_ref[t:t + 1, :] = x1_ref[t:t + 1, :] + o_t


def _peer_ffn(ids, gt, xn, x1, table):
    N, D = xn.shape
    TT = PEER_TOK_TILE
    nt = N // TT
    rows = TT * PEER_PICKS
    ids3 = ids.reshape(nt, 1, rows)
    return pl.pallas_call(
        _peer_ffn_kernel,
        grid=(nt,),
        in_specs=[
            pl.BlockSpec((1, 1, rows), lambda i: (i, 0, 0), memory_space=pltpu.SMEM),
            pl.BlockSpec((1, 1, rows), lambda i: (jnp.minimum(i + 1, nt - 1), 0, 0),
                         memory_space=pltpu.SMEM),
            pl.BlockSpec((TT, D), lambda i: (i, 0)),
            pl.BlockSpec((1, PEER_PICKS, PEER_ROUTE_TILE), lambda i: (i * TT // PEER_ROUTE_TILE, 0, 0)),
            pl.BlockSpec((TT, D), lambda i: (i, 0)),
            pl.BlockSpec(memory_space=pl.ANY),
        ],
        out_specs=pl.BlockSpec((TT, D), lambda i: (i, 0)),
        out_shape=jax.ShapeDtypeStruct((N, D), F32),
        scratch_shapes=[
            pltpu.VMEM((2, rows, D), jnp.int32),
            pltpu.SemaphoreType.DMA((2,)),
        ],
        compiler_params=pltpu.CompilerParams(
            dimension_semantics=("arbitrary",), vmem_limit_bytes=VMEM_LIMIT),
        name="peer_ffn",
    )(ids3, ids3, xn, gt, x1, table)


def _rope_partner(width):
    c = np.arange(width)
    j = c % ROPE_HALF
    return np.where(j < ROPE_QUARTER, c + ROPE_QUARTER, c - ROPE_QUARTER)


def _rope_tables(n):
    rows = n // GRID_W
    row = jnp.repeat(jnp.arange(rows, dtype=F32), GRID_W)
    col = jnp.tile(jnp.arange(GRID_W, dtype=F32), rows)
    inv = 1.0 / (ROPE_THETA ** (jnp.arange(0, ROPE_HALF, 2, dtype=F32) / ROPE_HALF))
    ang_r = row[:, None] * inv
    ang_c = col[:, None] * inv
    cos = jnp.concatenate([jnp.cos(ang_r)] * 2 + [jnp.cos(ang_c)] * 2, axis=-1)
    sin = jnp.concatenate([-jnp.sin(ang_r), jnp.sin(ang_r), -jnp.sin(ang_c), jnp.sin(ang_c)], axis=-1)
    return jnp.tile(cos, (1, N_Q_HEADS)), jnp.tile(sin, (1, N_Q_HEADS))


def _layer(x, p):
    B, n, D = x.shape
    N = B * n
    x2d = x.reshape(N, D)
    cos_t, sin_t = _rope_tables(n)
    yp, q, kt, v = _in_proj(x2d, n, p["g1"], p["w_ext"], p["pw"], p["gq"], p["gqs"], p["gk"],
                            p["gks"], cos_t, sin_t, p["mavg"])
    attn = _attention(q.reshape(B, n, ATTN_WIDTH), kt, v.reshape(B, n, KV_WIDTH))
    x1, xn = _out_proj(x2d, n, yp, attn.reshape(N, ATTN_WIDTH), p["scale"], p["w_out"], p["g2"])
    ids, gt = _peer_route(xn, p["wqt"], p["sk"])
    y = _peer_ffn(ids, gt, xn, x1, p["table"])
    return y.reshape(B, n, D)


def _prepare(norm1_g, w_in, pool_w, pool_scale, q_norm_g, k_norm_g, w_out, norm2_g,
             peer_wq, peer_subkeys, peer_u, peer_v):
    pq = _rope_partner(ATTN_WIDTH)
    pk = _rope_partner(KV_WIDTH)
    o_q = POOL_WIDTH
    o_k = o_q + ATTN_WIDTH
    w_ext = jnp.concatenate([w_in, w_in[:, o_q + pq], w_in[:, o_k + pk]], axis=1).astype(BF16)
    gq = jnp.tile(q_norm_g, N_Q_HEADS)
    gk = jnp.tile(k_norm_g, N_KV_HEADS)
    blk = np.arange(ATTN_WIDTH) // HEAD_DIM
    mavg = jnp.asarray((blk[:, None] == blk[None, :]) / HEAD_DIM, BF16)
    ub = lax.bitcast_convert_type(peer_u.astype(BF16), jnp.uint16).astype(jnp.uint32)
    vb = lax.bitcast_convert_type(peer_v.astype(BF16), jnp.uint16).astype(jnp.uint32)
    table = lax.bitcast_convert_type((ub << 16) | vb, jnp.int32)
    return dict(
        g1=norm1_g[None, :], w_ext=w_ext, pw=pool_w.astype(BF16),
        gq=gq[None, :], gqs=gq[pq][None, :], gk=gk[None, :], gks=gk[pk][None, :], mavg=mavg,
        scale=pool_scale[None, :], w_out=w_out.astype(BF16), g2=norm2_g[None, :],
        wqt=peer_wq.T.astype(BF16),
        sk=peer_subkeys.reshape(PEER_HEADS * 2, PEER_NKEYS, PEER_HALF).astype(BF16),
        table=table,
    )


def kernel(x_prompt, x_sample, norm1_g, w_in, pool_w, pool_scale, q_norm_g, k_norm_g, w_out,
           norm2_g, peer_wq, peer_subkeys, peer_u, peer_v):
    y_prompt, y_sample = x_prompt, x_sample
    for l in range(norm1_g.shape[0]):
        p = _prepare(norm1_g[l], w_in[l], pool_w[l], pool_scale[l], q_norm_g[l], k_norm_g[l],
                     w_out[l], norm2_g[l], peer_wq[l], peer_subkeys[l], peer_u[l], peer_v[l])
        y_prompt = _layer(y_prompt, p)
        y_sample = _layer(y_sample, p)
    return (y_prompt, y_sample)
```

```python
import functools
import math

import numpy as np
import jax
import jax.numpy as jnp
from jax import lax
from jax.experimental import pallas as pl
from jax.experimental.pallas import tpu as pltpu

F32 = jnp.float32
BF16 = jnp.bfloat16

EPS = 1e-6
GRID_W = 64
POOL_WINDOWS = (2, 4, 8, 16)
POOL_GROUP_DIM = 128
POOL_WIDTH = 512
POOL_HALO = 8
HEAD_DIM = 64
N_Q_HEADS = 8
N_KV_HEADS = 2
GQA_GROUP = N_Q_HEADS // N_KV_HEADS
ATTN_WIDTH = N_Q_HEADS * HEAD_DIM
KV_WIDTH = N_KV_HEADS * HEAD_DIM
ROPE_HALF = HEAD_DIM // 2
ROPE_QUARTER = ROPE_HALF // 2
ROPE_THETA = 10000.0
ATTN_SCALE = 1.0 / math.sqrt(HEAD_DIM)
PEER_HEADS = 8
PEER_NKEYS = 128
PEER_HALF = 128
PEER_TOPK = 16
PEER_PICKS = PEER_HEADS * PEER_TOPK

VMEM_LIMIT = 48 * 1024 * 1024


def _tile(n, want):
    t = min(n, want)
    assert n % t == 0, (n, t)
    return t


def _head_mean_square(a, mavg):
    sq = a * a
    hi = sq.astype(BF16)
    lo = (sq - hi.astype(F32)).astype(BF16)
    return (jnp.dot(hi, mavg, preferred_element_type=F32)
            + jnp.dot(lo, mavg, preferred_element_type=F32))


def _in_proj_kernel(x_ref, g1_ref, w_ref, pw_ref, gq_ref, gqs_ref, gk_ref, gks_ref,
                    cos_ref, sin_ref, mavg_ref, yp_ref, q_ref, kt_ref, v_ref):
    x = x_ref[...]
    ms = jnp.mean(x * x, axis=-1, keepdims=True)
    h = (x * lax.rsqrt(ms + EPS) * g1_ref[...]).astype(BF16)
    z = jnp.dot(h, w_ref[...], preferred_element_type=F32)
    for g in range(len(POOL_WINDOWS)):
        sl = slice(g * POOL_GROUP_DIM, (g + 1) * POOL_GROUP_DIM)
        yp_ref[:, sl] = jnp.dot(z[:, sl].astype(BF16), pw_ref[g], preferred_element_type=F32)
    o_q = POOL_WIDTH
    o_k = o_q + ATTN_WIDTH
    o_v = o_k + KV_WIDTH
    o_qs = o_v + KV_WIDTH
    o_ks = o_qs + ATTN_WIDTH
    zq, zk, zv = z[:, o_q:o_k], z[:, o_k:o_v], z[:, o_v:o_qs]
    zqs, zks = z[:, o_qs:o_ks], z[:, o_ks:o_ks + KV_WIDTH]
    cos = cos_ref[...]
    sin = sin_ref[...]
    rq = lax.rsqrt(_head_mean_square(zq, mavg_ref[...]) + EPS)
    q = ((zq * rq * gq_ref[...]) * cos + (zqs * rq * gqs_ref[...]) * sin) * ATTN_SCALE
    q_ref[...] = q.astype(BF16)
    rk = lax.rsqrt(_head_mean_square(zk, mavg_ref[:KV_WIDTH, :KV_WIDTH]) + EPS)
    k = (zk * rk * gk_ref[...]) * cos[:, :KV_WIDTH] + (zks * rk * gks_ref[...]) * sin[:, :KV_WIDTH]
    kt_ref[0] = k.T.astype(BF16)
    v_ref[...] = zv.astype(BF16)


def _in_proj(x2d, n, g1, w_ext, pw, gq, gqs, gk, gks, cos_t, sin_t, mavg):
    N, D = x2d.shape
    T = _tile(n, 512)
    tiles_per_seq = n // T
    const = lambda *s: pl.BlockSpec(s, lambda i: (0,) * len(s))
    return pl.pallas_call(
        _in_proj_kernel,
        grid=(N // T,),
        in_specs=[
            pl.BlockSpec((T, D), lambda i: (i, 0)),
            const(1, D), const(*w_ext.shape), const(*pw.shape),
            const(1, ATTN_WIDTH), const(1, ATTN_WIDTH), const(1, KV_WIDTH), const(1, KV_WIDTH),
            pl.BlockSpec((T, ATTN_WIDTH), lambda i: (i % tiles_per_seq, 0)),
            pl.BlockSpec((T, ATTN_WIDTH), lambda i: (i % tiles_per_seq, 0)),
            const(ATTN_WIDTH, ATTN_WIDTH),
        ],
        out_specs=[
            pl.BlockSpec((T, POOL_WIDTH), lambda i: (i, 0)),
            pl.BlockSpec((T, ATTN_WIDTH), lambda i: (i, 0)),
            pl.BlockSpec((1, KV_WIDTH, T), lambda i: (i // tiles_per_seq, 0, i % tiles_per_seq)),
            pl.BlockSpec((T, KV_WIDTH), lambda i: (i, 0)),
        ],
        out_shape=[
            jax.ShapeDtypeStruct((N, POOL_WIDTH), F32),
            jax.ShapeDtypeStruct((N, ATTN_WIDTH), BF16),
            jax.ShapeDtypeStruct((N // n, KV_WIDTH, n), BF16),
            jax.ShapeDtypeStruct((N, KV_WIDTH), BF16),
        ],
        compiler_params=pltpu.CompilerParams(
            dimension_semantics=("arbitrary",), vmem_limit_bytes=VMEM_LIMIT),
        name="in_proj",
    )(x2d, g1, w_ext, pw, gq, gqs, gk, gks, cos_t, sin_t, mavg)


def _attn_kernel(q_ref, kt_ref, v_ref, o_ref):
    outs = []
    for j in range(N_KV_HEADS):
        kt = kt_ref[0, j * HEAD_DIM:(j + 1) * HEAD_DIM, :]
        v = v_ref[0, :, j * HEAD_DIM:(j + 1) * HEAD_DIM]
        for g in range(GQA_GROUP):
            hq = j * GQA_GROUP + g
            q = q_ref[0, :, hq * HEAD_DIM:(hq + 1) * HEAD_DIM]
            s = jnp.dot(q, kt, preferred_element_type=F32)
            m = jnp.max(s, axis=-1, keepdims=True)
            p = jnp.exp(s - m)
            l = jnp.sum(p, axis=-1, keepdims=True)
            o = jnp.dot(p.astype(BF16), v, preferred_element_type=F32)
            outs.append(o / l)
    o_ref[0] = jnp.concatenate(outs, axis=-1).astype(BF16)


def _attention(q, kt, v):
    B, n, _ = q.shape
    tq = _tile(n, 256)
    return pl.pallas_call(
        _attn_kernel,
        grid=(B, n // tq),
        in_specs=[
            pl.BlockSpec((1, tq, ATTN_WIDTH), lambda b, i: (b, i, 0)),
            pl.BlockSpec((1, KV_WIDTH, n), lambda b, i: (b, 0, 0)),
            pl.BlockSpec((1, n, KV_WIDTH), lambda b, i: (b, 0, 0)),
        ],
        out_specs=pl.BlockSpec((1, tq, ATTN_WIDTH), lambda b, i: (b, i, 0)),
        out_shape=jax.ShapeDtypeStruct((B, n, ATTN_WIDTH), BF16),
        compiler_params=pltpu.CompilerParams(
            dimension_semantics=("arbitrary", "arbitrary"), vmem_limit_bytes=VMEM_LIMIT),
        name="attention",
    )(q, kt, v)


def _out_proj_kernel(n, x_ref, yp_ref, prev_ref, next_ref, at_ref, sc_ref, w_ref, g2_ref,
                     x1_ref, xn_ref):
    T = x_ref.shape[0]
    tiles_per_seq = n // T
    si = pl.program_id(0) % tiles_per_seq
    not_first = (si > 0).astype(F32)
    not_last = (si < tiles_per_seq - 1).astype(F32)
    cur = yp_ref[...]
    ext = jnp.concatenate([prev_ref[...] * not_first, cur, next_ref[...] * not_last], axis=0)
    t = si * T + lax.broadcasted_iota(jnp.int32, (T, 1), 0)
    pooled = []
    for g, w in enumerate(POOL_WINDOWS):
        sl = slice(g * POOL_GROUP_DIM, (g + 1) * POOL_GROUP_DIM)
        eg = ext[:, sl]
        acc = eg[POOL_HALO - w // 2:POOL_HALO - w // 2 + T]
        for d in range(-w // 2 + 1, w - w // 2):
            acc = acc + eg[POOL_HALO + d:POOL_HALO + d + T]
        cnt = jnp.minimum(t + (w - w // 2), n) - jnp.maximum(t - w // 2, 0)
        pooled.append(acc / cnt.astype(F32) - cur[:, sl])
    pool = jnp.concatenate(pooled, axis=-1) * sc_ref[...]
    mix = jnp.concatenate([pool.astype(BF16), at_ref[...]], axis=-1)
    x1 = x_ref[...] + jnp.dot(mix, w_ref[...], preferred_element_type=F32)
    x1_ref[...] = x1
    ms = jnp.mean(x1 * x1, axis=-1, keepdims=True)
    xn_ref[...] = (x1 * lax.rsqrt(ms + EPS) * g2_ref[...]).astype(BF16)


def _out_proj(x2d, n, yp, attn, scale, w_out, g2):
    N, D = x2d.shape
    T = _tile(n, 512)
    hb = T // POOL_HALO
    last_hb = N // POOL_HALO - 1
    const = lambda *s: pl.BlockSpec(s, lambda i: (0,) * len(s))
    return pl.pallas_call(
        functools.partial(_out_proj_kernel, n),
        grid=(N // T,),
        in_specs=[
            pl.BlockSpec((T, D), lambda i: (i, 0)),
            pl.BlockSpec((T, POOL_WIDTH), lambda i: (i, 0)),
            pl.BlockSpec((POOL_HALO, POOL_WIDTH), lambda i: (jnp.maximum(i * hb - 1, 0), 0)),
            pl.BlockSpec((POOL_HALO, POOL_WIDTH), lambda i: (jnp.minimum((i + 1) * hb, last_hb), 0)),
            pl.BlockSpec((T, ATTN_WIDTH), lambda i: (i, 0)),
            const(1, POOL_WIDTH), const(*w_out.shape), const(1, D),
        ],
        out_specs=[
            pl.BlockSpec((T, D), lambda i: (i, 0)),
            pl.BlockSpec((T, D), lambda i: (i, 0)),
        ],
        out_shape=[
            jax.ShapeDtypeStruct((N, D), F32),
            jax.ShapeDtypeStruct((N, D), BF16),
        ],
        compiler_params=pltpu.CompilerParams(
            dimension_semantics=("arbitrary",), vmem_limit_bytes=VMEM_LIMIT),
        name="out_proj",
    )(x2d, yp, yp, yp, attn, scale, w_out, g2)


def _extract_top(vals, payload, k):
    R = vals.shape[0]
    rows = lax.broadcasted_iota(jnp.int32, vals.shape, 0)
    top_v, top_p = [], []
    for _ in range(k):
        m = jnp.max(vals, axis=0, keepdims=True)
        pos = jnp.min(jnp.where(vals == m, rows, R), axis=0, keepdims=True)
        sel = rows == pos
        top_v.append(m)
        if payload is None:
            top_p.append(pos)
        else:
            top_p.append(jnp.max(jnp.where(sel, payload, -1), axis=0, keepdims=True))
        vals = jnp.where(sel, -jnp.inf, vals)
    return jnp.concatenate(top_v, axis=0), jnp.concatenate(top_p, axis=0)


def _pair_candidates(v0, i0, v1, i1):
    K = PEER_TOPK
    sub = lax.broadcasted_iota(jnp.int32, (8, v0.shape[1]), 0)
    vals = [v0[0:1] + v1, v0[1:2] + v1[0:8]]
    eids = [i0[0:1] * PEER_NKEYS + i1, i0[1:2] * PEER_NKEYS + i1[0:8]]
    for a in range(2, 8):
        keep = sub < K // (a + 1)
        vals.append(jnp.where(keep, v0[a:a + 1] + v1[0:8], -jnp.inf))
        eids.append(i0[a:a + 1] * PEER_NKEYS + i1[0:8])
    vals.append(v0[8:16] + v1[0:1])
    eids.append(i0[8:16] * PEER_NKEYS + i1[0:1])
    return jnp.concatenate(vals, axis=0), jnp.concatenate(eids, axis=0)


def _peer_route_kernel(xn_ref, wqt_ref, sk_ref, e_ref, gt_ref, qt_scr, et_scr):
    T = xn_ref.shape[0]
    qt_scr[...] = lax.dot_general(wqt_ref[...], xn_ref[...], (((1,), (1,)), ((), ())),
                                  preferred_element_type=F32).astype(BF16)
    def head_body(h, carry):
        sub_v, sub_i = [], []
        for p in range(2):
            hp = h * 2 + p
            qhp = qt_scr[pl.ds(pl.multiple_of(hp * PEER_HALF, PEER_HALF), PEER_HALF), :]
            s = jnp.dot(sk_ref[hp], qhp, preferred_element_type=F32)
            tv, ti = _extract_top(s, None, PEER_TOPK)
            sub_v.append(tv)
            sub_i.append(ti)
        comb, eid = _pair_candidates(sub_v[0], sub_i[0], sub_v[1], sub_i[1])
        cv, ce = _extract_top(comb, eid, PEER_TOPK)
        ex = jnp.exp(cv - cv[0:1, :])
        gate = ex / jnp.sum(ex, axis=0, keepdims=True)
        row0 = pl.multiple_of(h * PEER_TOPK, PEER_TOPK)
        et_scr[pl.ds(row0, PEER_TOPK), :] = ce.astype(F32)
        gt_ref[0, pl.ds(row0, PEER_TOPK), :] = gate
        return carry

    lax.fori_loop(0, PEER_HEADS, head_body, 0)
    e_ref[...] = et_scr[...].T.astype(jnp.int32)


PEER_ROUTE_TILE = 128


def _peer_route(xn, wqt, sk):
    N, D = xn.shape
    T = PEER_ROUTE_TILE
    const = lambda *s: pl.BlockSpec(s, lambda i: (0,) * len(s))
    return pl.pallas_call(
        _peer_route_kernel,
        grid=(N // T,),
        in_specs=[pl.BlockSpec((T, D), lambda i: (i, 0)), const(*wqt.shape), const(*sk.shape)],
        out_specs=[
            pl.BlockSpec((T, PEER_PICKS), lambda i: (i, 0)),
            pl.BlockSpec((1, PEER_PICKS, T), lambda i: (i, 0, 0)),
        ],
        out_shape=[
            jax.ShapeDtypeStruct((N, PEER_PICKS), jnp.int32),
            jax.ShapeDtypeStruct((N // T, PEER_PICKS, T), F32),
        ],
        scratch_shapes=[
            pltpu.VMEM((wqt.shape[0], T), BF16),
            pltpu.VMEM((PEER_PICKS, T), F32),
        ],
        compiler_params=pltpu.CompilerParams(
            dimension_semantics=("arbitrary",), vmem_limit_bytes=VMEM_LIMIT),
        name="peer_route",
    )(xn, wqt, sk)


PEER_TOK_TILE = 16
PEER_PROLOGUE_UNROLL = 16
ROW_SUBLANES = 8
ISSUE_PER_GROUP = 4
U_HALF_MASK = -65536


def _sum_sublanes_of_8(ps, sub):
    def comb(a, b, h):
        m = (sub & h) == 0
        if 2 * h == ROW_SUBLANES:
            return jnp.where(m, a, b) + pltpu.roll(jnp.where(m, b, a), h, 0)
        return (jnp.where(m, a, pltpu.roll(b, h, 0))
                + jnp.where(m, pltpu.roll(a, ROW_SUBLANES - h, 0), b))
    l1 = [comb(ps[2 * j], ps[2 * j + 1], 1) for j in range(4)]
    l2 = [comb(l1[0], l1[1], 2), comb(l1[2], l1[3], 2)]
    return comb(l2[0], l2[1], 4)


def _peer_ffn_kernel(ids_cur_ref, ids_nxt_ref, x1_ref, g2_ref, gt_ref, tab_ref, y_ref,
                     buf_a, buf_b, hs_scr, ab_scr, sem):
    TT = PEER_TOK_TILE
    rows = TT * PEER_PICKS
    D = ROW_SUBLANES * 128
    i = pl.program_id(0)
    sub = lax.broadcasted_iota(jnp.int32, (ROW_SUBLANES, 128), 0)
    lane_ids = lax.broadcasted_iota(jnp.int32, (PEER_PICKS, PEER_ROUTE_TILE), 1)
    g2 = g2_ref[...]

    def start_row(ids_ref, j_ids, dst, dst_sem, j, prio):
        pltpu.make_async_copy(tab_ref.at[ids_ref[0, 0, j_ids]], dst.at[j], dst_sem).start(priority=prio)

    def wait_tile(dst, dst_sem):
        pltpu.make_async_copy(dst, dst, dst_sem).wait()

    def process(cur, cur_sem, nxt, nxt_sem, nxt_ids_ref, nxt_ids_off, tok0):
        wait_tile(cur, cur_sem)
        lanes = lane_ids - ((i * 2 * TT) % PEER_ROUTE_TILE + tok0)

        for t in range(TT):
            base = t * PEER_PICKS
            x1 = x1_ref[tok0 + t]
            ms = jnp.sum(x1 * x1, axis=(0, 1), keepdims=True) * (1.0 / D)
            xt = x1 * lax.rsqrt(ms + EPS) * g2
            groups = []
            for g in range(PEER_PICKS // ROW_SUBLANES):
                for k in range(g * ISSUE_PER_GROUP, (g + 1) * ISSUE_PER_GROUP):
                    start_row(nxt_ids_ref, nxt_ids_off + base + k, nxt, nxt_sem, base + k, k % 2)
                ps = []
                for j in range(ROW_SUBLANES):
                    w = cur[base + g * ROW_SUBLANES + j]
                    ps.append(lax.bitcast_convert_type(w & jnp.int32(U_HALF_MASK), F32) * xt)
                groups.append(_sum_sublanes_of_8(ps, sub))
            hs_scr[t] = jnp.concatenate(groups, axis=0)

        hid = jnp.zeros((PEER_PICKS, PEER_ROUTE_TILE), F32)
        for t in range(TT):
            hid = jnp.where(lanes == t, jnp.sum(hs_scr[t], axis=-1, keepdims=True), hid)
        act = 0.5 * hid * (1.0 + lax.erf(hid * (1.0 / math.sqrt(2.0)))) * gt_ref[0]
        for t in range(TT):
            a_t = jnp.sum(jnp.where(lanes == t, act, 0.0), axis=-1, keepdims=True)
            ab_scr[t] = jnp.broadcast_to(a_t, ab_scr.shape[1:])

        for t in range(TT):
            base = t * PEER_PICKS
            accs = [jnp.zeros((ROW_SUBLANES, 128), F32) for _ in range(4)]
            for k in range(PEER_PICKS):
                if k % 2 == 0:
                    kn = PEER_PICKS // 2 + k // 2
                    start_row(nxt_ids_ref, nxt_ids_off + base + kn, nxt, nxt_sem, base + kn, kn % 2)
                gv = lax.bitcast_convert_type(cur[base + k] << 16, F32)
                accs[k % 4] = accs[k % 4] + gv * jnp.broadcast_to(ab_scr[t, k:k + 1, :], gv.shape)
            y_ref[tok0 + t] = x1_ref[tok0 + t] + ((accs[0] + accs[1]) + (accs[2] + accs[3]))

    @pl.when(i == 0)
    def _():
        def body(c, carry):
            for r in range(PEER_PROLOGUE_UNROLL):
                j = c * PEER_PROLOGUE_UNROLL + r
                start_row(ids_cur_ref, j, buf_a, sem.at[0], j, r % 2)
            return carry
        lax.fori_loop(0, rows // PEER_PROLOGUE_UNROLL, body, 0)

    process(buf_a, sem.at[0], buf_b, sem.at[1], ids_cur_ref, rows, 0)
    process(buf_b, sem.at[1], buf_a, sem.at[0], ids_nxt_ref, 0, TT)

    @pl.when(i == pl.num_programs(0) - 1)
    def _():
        wait_tile(buf_a, sem.at[0])


def _peer_ffn(ids, gt, x1_rows, g2_rows, table):
    N = x1_rows.shape[0]
    TT = PEER_TOK_TILE
    ns = N // (2 * TT)
    rows = TT * PEER_PICKS
    ids3 = ids.reshape(ns, 1, 2 * rows)
    tok = pl.BlockSpec((2 * TT, ROW_SUBLANES, 128), lambda i: (i, 0, 0))
    return pl.pallas_call(
        _peer_ffn_kernel,
        grid=(ns,),
        in_specs=[
            pl.BlockSpec((1, 1, 2 * rows), lambda i: (i, 0, 0), memory_space=pltpu.SMEM),
            pl.BlockSpec((1, 1, 2 * rows), lambda i: (jnp.minimum(i + 1, ns - 1), 0, 0),
                         memory_space=pltpu.SMEM),
            tok,
            pl.BlockSpec((ROW_SUBLANES, 128), lambda i: (0, 0)),
            pl.BlockSpec((1, PEER_PICKS, PEER_ROUTE_TILE),
                         lambda i: (i * 2 * TT // PEER_ROUTE_TILE, 0, 0)),
            pl.BlockSpec(memory_space=pl.ANY),
        ],
        out_specs=tok,
        out_shape=jax.ShapeDtypeStruct(x1_rows.shape, F32),
        scratch_shapes=[
            pltpu.VMEM((rows, ROW_SUBLANES, 128), jnp.int32),
            pltpu.VMEM((rows, ROW_SUBLANES, 128), jnp.int32),
            pltpu.VMEM((TT, PEER_PICKS, 128), F32),
            pltpu.VMEM((TT, PEER_PICKS, 128), F32),
            pltpu.SemaphoreType.DMA((2,)),
        ],
        compiler_params=pltpu.CompilerParams(
            dimension_semantics=("arbitrary",), vmem_limit_bytes=VMEM_LIMIT),
        name="peer_ffn",
    )(ids3, ids3, x1_rows, g2_rows, gt, table)


def _rope_partner(width):
    c = np.arange(width)
    j = c % ROPE_HALF
    return np.where(j < ROPE_QUARTER, c + ROPE_QUARTER, c - ROPE_QUARTER)


def _rope_tables(n):
    rows = n // GRID_W
    row = jnp.repeat(jnp.arange(rows, dtype=F32), GRID_W)
    col = jnp.tile(jnp.arange(GRID_W, dtype=F32), rows)
    inv = 1.0 / (ROPE_THETA ** (jnp.arange(0, ROPE_HALF, 2, dtype=F32) / ROPE_HALF))
    ang_r = row[:, None] * inv
    ang_c = col[:, None] * inv
    cos = jnp.concatenate([jnp.cos(ang_r)] * 2 + [jnp.cos(ang_c)] * 2, axis=-1)
    sin = jnp.concatenate([-jnp.sin(ang_r), jnp.sin(ang_r), -jnp.sin(ang_c), jnp.sin(ang_c)], axis=-1)
    return jnp.tile(cos, (1, N_Q_HEADS)), jnp.tile(sin, (1, N_Q_HEADS))


def _layer(x, p):
    B, n, D = x.shape
    N = B * n
    x2d = x.reshape(N, D)
    cos_t, sin_t = _rope_tables(n)
    yp, q, kt, v = _in_proj(x2d, n, p["g1"], p["w_ext"], p["pw"], p["gq"], p["gqs"], p["gk"],
                            p["gks"], cos_t, sin_t, p["mavg"])
    attn = _attention(q.reshape(B, n, ATTN_WIDTH), kt, v.reshape(B, n, KV_WIDTH))
    x1, xn = _out_proj(x2d, n, yp, attn.reshape(N, ATTN_WIDTH), p["scale"], p["w_out"], p["g2"])
    ids, gt = _peer_route(xn, p["wqt"], p["sk"])
    y = _peer_ffn(ids, gt, x1.reshape(N, ROW_SUBLANES, 128), p["g2"].reshape(ROW_SUBLANES, 128),
                  p["table"])
    return y.reshape(B, n, D)


def _prepare(norm1_g, w_in, pool_w, pool_scale, q_norm_g, k_norm_g, w_out, norm2_g,
             peer_wq, peer_subkeys, peer_u, peer_v):
    pq = _rope_partner(ATTN_WIDTH)
    pk = _rope_partner(KV_WIDTH)
    o_q = POOL_WIDTH
    o_k = o_q + ATTN_WIDTH
    w_ext = jnp.concatenate([w_in, w_in[:, o_q + pq], w_in[:, o_k + pk]], axis=1).astype(BF16)
    gq = jnp.tile(q_norm_g, N_Q_HEADS)
    gk = jnp.tile(k_norm_g, N_KV_HEADS)
    blk = np.arange(ATTN_WIDTH) // HEAD_DIM
    mavg = jnp.asarray((blk[:, None] == blk[None, :]) / HEAD_DIM, BF16)
    ub = lax.bitcast_convert_type(peer_u.astype(BF16), jnp.uint16).astype(jnp.uint32)
    vb = lax.bitcast_convert_type(peer_v.astype(BF16), jnp.uint16).astype(jnp.uint32)
    table = lax.bitcast_convert_type((ub << 16) | vb, jnp.int32).reshape(-1, ROW_SUBLANES, 128)
    return dict(
        g1=norm1_g[None, :], w_ext=w_ext, pw=pool_w.astype(BF16),
        gq=gq[None, :], gqs=gq[pq][None, :], gk=gk[None, :], gks=gk[pk][None, :], mavg=mavg,
        scale=pool_scale[None, :], w_out=w_out.astype(BF16), g2=norm2_g[None, :],
        wqt=peer_wq.T.astype(BF16),
        sk=peer_subkeys.reshape(PEER_HEADS * 2, PEER_NKEYS, PEER_HALF).astype(BF16),
        table=table,
    )


def kernel(x_prompt, x_sample, norm1_g, w_in, pool_w, pool_scale, q_norm_g, k_norm_g, w_out,
           norm2_g, peer_wq, peer_subkeys, peer_u, peer_v):
    y_prompt, y_sample = x_prompt, x_sample
    for l in range(norm1_g.shape[0]):
        p = _prepare(norm1_g[l], w_in[l], pool_w[l], pool_scale[l], q_norm_g[l], k_norm_g[l],
                     w_out[l], norm2_g[l], peer_wq[l], peer_subkeys[l], peer_u[l], peer_v[l])
        y_prompt = _layer(y_prompt, p)
        y_sample = _layer(y_sample, p)
    return (y_prompt, y_sample)
```

```python
import functools
import math

import numpy as np
import jax
import jax.numpy as jnp
from jax import lax
from jax.experimental import pallas as pl
from jax.experimental.pallas import tpu as pltpu

F32 = jnp.float32
BF16 = jnp.bfloat16

EPS = 1e-6
GRID_W = 64
POOL_WINDOWS = (2, 4, 8, 16)
POOL_GROUP_DIM = 128
POOL_WIDTH = 512
POOL_HALO = 8
HEAD_DIM = 64
N_Q_HEADS = 8
N_KV_HEADS = 2
GQA_GROUP = N_Q_HEADS // N_KV_HEADS
ATTN_WIDTH = N_Q_HEADS * HEAD_DIM
KV_WIDTH = N_KV_HEADS * HEAD_DIM
ROPE_HALF = HEAD_DIM // 2
ROPE_QUARTER = ROPE_HALF // 2
ROPE_THETA = 10000.0
ATTN_SCALE = 1.0 / math.sqrt(HEAD_DIM)
PEER_HEADS = 8
PEER_NKEYS = 128
PEER_HALF = 128
PEER_TOPK = 16
PEER_PICKS = PEER_HEADS * PEER_TOPK

VMEM_LIMIT = 48 * 1024 * 1024


def _tile(n, want):
    t = min(n, want)
    assert n % t == 0, (n, t)
    return t


def _head_mean_square(a, mavg):
    sq = a * a
    hi = sq.astype(BF16)
    lo = (sq - hi.astype(F32)).astype(BF16)
    return (jnp.dot(hi, mavg, preferred_element_type=F32)
            + jnp.dot(lo, mavg, preferred_element_type=F32))


def _in_proj_kernel(x_ref, g1_ref, w_ref, pw_ref, gq_ref, gqs_ref, gk_ref, gks_ref,
                    cos_ref, sin_ref, mavg_ref, yp_ref, q_ref, kt_ref, v_ref):
    x = x_ref[...]
    ms = jnp.mean(x * x, axis=-1, keepdims=True)
    h = (x * lax.rsqrt(ms + EPS) * g1_ref[...]).astype(BF16)
    z = jnp.dot(h, w_ref[...], preferred_element_type=F32)
    for g in range(len(POOL_WINDOWS)):
        sl = slice(g * POOL_GROUP_DIM, (g + 1) * POOL_GROUP_DIM)
        yp_ref[:, sl] = jnp.dot(z[:, sl].astype(BF16), pw_ref[g], preferred_element_type=F32)
    o_q = POOL_WIDTH
    o_k = o_q + ATTN_WIDTH
    o_v = o_k + KV_WIDTH
    o_qs = o_v + KV_WIDTH
    o_ks = o_qs + ATTN_WIDTH
    zq, zk, zv = z[:, o_q:o_k], z[:, o_k:o_v], z[:, o_v:o_qs]
    zqs, zks = z[:, o_qs:o_ks], z[:, o_ks:o_ks + KV_WIDTH]
    cos = cos_ref[...]
    sin = sin_ref[...]
    rq = lax.rsqrt(_head_mean_square(zq, mavg_ref[...]) + EPS)
    q = ((zq * rq * gq_ref[...]) * cos + (zqs * rq * gqs_ref[...]) * sin) * ATTN_SCALE
    q_ref[...] = q.astype(BF16)
    rk = lax.rsqrt(_head_mean_square(zk, mavg_ref[:KV_WIDTH, :KV_WIDTH]) + EPS)
    k = (zk * rk * gk_ref[...]) * cos[:, :KV_WIDTH] + (zks * rk * gks_ref[...]) * sin[:, :KV_WIDTH]
    kt_ref[0] = k.T.astype(BF16)
    v_ref[...] = zv.astype(BF16)


def _in_proj(x2d, n, g1, w_ext, pw, gq, gqs, gk, gks, cos_t, sin_t, mavg):
    N, D = x2d.shape
    T = _tile(n, 512)
    tiles_per_seq = n // T
    const = lambda *s: pl.BlockSpec(s, lambda i: (0,) * len(s))
    return pl.pallas_call(
        _in_proj_kernel,
        grid=(N // T,),
        in_specs=[
            pl.BlockSpec((T, D), lambda i: (i, 0)),
            const(1, D), const(*w_ext.shape), const(*pw.shape),
            const(1, ATTN_WIDTH), const(1, ATTN_WIDTH), const(1, KV_WIDTH), const(1, KV_WIDTH),
            pl.BlockSpec((T, ATTN_WIDTH), lambda i: (i % tiles_per_seq, 0)),
            pl.BlockSpec((T, ATTN_WIDTH), lambda i: (i % tiles_per_seq, 0)),
            const(ATTN_WIDTH, ATTN_WIDTH),
        ],
        out_specs=[
            pl.BlockSpec((T, POOL_WIDTH), lambda i: (i, 0)),
            pl.BlockSpec((T, ATTN_WIDTH), lambda i: (i, 0)),
            pl.BlockSpec((1, KV_WIDTH, T), lambda i: (i // tiles_per_seq, 0, i % tiles_per_seq)),
            pl.BlockSpec((T, KV_WIDTH), lambda i: (i, 0)),
        ],
        out_shape=[
            jax.ShapeDtypeStruct((N, POOL_WIDTH), F32),
            jax.ShapeDtypeStruct((N, ATTN_WIDTH), BF16),
            jax.ShapeDtypeStruct((N // n, KV_WIDTH, n), BF16),
            jax.ShapeDtypeStruct((N, KV_WIDTH), BF16),
        ],
        compiler_params=pltpu.CompilerParams(
            dimension_semantics=("arbitrary",), vmem_limit_bytes=VMEM_LIMIT),
        name="in_proj",
    )(x2d, g1, w_ext, pw, gq, gqs, gk, gks, cos_t, sin_t, mavg)


def _attn_kernel(q_ref, kt_ref, v_ref, o_ref):
    outs = []
    for j in range(N_KV_HEADS):
        kt = kt_ref[0, j * HEAD_DIM:(j + 1) * HEAD_DIM, :]
        v = v_ref[0, :, j * HEAD_DIM:(j + 1) * HEAD_DIM]
        for g in range(GQA_GROUP):
            hq = j * GQA_GROUP + g
            q = q_ref[0, :, hq * HEAD_DIM:(hq + 1) * HEAD_DIM]
            s = jnp.dot(q, kt, preferred_element_type=F32)
            m = jnp.max(s, axis=-1, keepdims=True)
            p = jnp.exp(s - m)
            l = jnp.sum(p, axis=-1, keepdims=True)
            o = jnp.dot(p.astype(BF16), v, preferred_element_type=F32)
            outs.append(o / l)
    o_ref[0] = jnp.concatenate(outs, axis=-1).astype(BF16)


def _attention(q, kt, v):
    B, n, _ = q.shape
    tq = _tile(n, 256)
    return pl.pallas_call(
        _attn_kernel,
        grid=(B, n // tq),
        in_specs=[
            pl.BlockSpec((1, tq, ATTN_WIDTH), lambda b, i: (b, i, 0)),
            pl.BlockSpec((1, KV_WIDTH, n), lambda b, i: (b, 0, 0)),
            pl.BlockSpec((1, n, KV_WIDTH), lambda b, i: (b, 0, 0)),
        ],
        out_specs=pl.BlockSpec((1, tq, ATTN_WIDTH), lambda b, i: (b, i, 0)),
        out_shape=jax.ShapeDtypeStruct((B, n, ATTN_WIDTH), BF16),
        compiler_params=pltpu.CompilerParams(
            dimension_semantics=("arbitrary", "arbitrary"), vmem_limit_bytes=VMEM_LIMIT),
        name="attention",
    )(q, kt, v)


def _out_proj_kernel(n, x_ref, yp_ref, prev_ref, next_ref, at_ref, sc_ref, w_ref, g2_ref,
                     x1_ref, xn_ref):
    T = x_ref.shape[0]
    tiles_per_seq = n // T
    si = pl.program_id(0) % tiles_per_seq
    not_first = (si > 0).astype(F32)
    not_last = (si < tiles_per_seq - 1).astype(F32)
    cur = yp_ref[...]
    ext = jnp.concatenate([prev_ref[...] * not_first, cur, next_ref[...] * not_last], axis=0)
    t = si * T + lax.broadcasted_iota(jnp.int32, (T, 1), 0)
    pooled = []
    for g, w in enumerate(POOL_WINDOWS):
        sl = slice(g * POOL_GROUP_DIM, (g + 1) * POOL_GROUP_DIM)
        eg = ext[:, sl]
        acc = eg[POOL_HALO - w // 2:POOL_HALO - w // 2 + T]
        for d in range(-w // 2 + 1, w - w // 2):
            acc = acc + eg[POOL_HALO + d:POOL_HALO + d + T]
        cnt = jnp.minimum(t + (w - w // 2), n) - jnp.maximum(t - w // 2, 0)
        pooled.append(acc / cnt.astype(F32) - cur[:, sl])
    pool = jnp.concatenate(pooled, axis=-1) * sc_ref[...]
    mix = jnp.concatenate([pool.astype(BF16), at_ref[...]], axis=-1)
    x1 = x_ref[...] + jnp.dot(mix, w_ref[...], preferred_element_type=F32)
    x1_ref[...] = x1
    ms = jnp.mean(x1 * x1, axis=-1, keepdims=True)
    xn_ref[...] = (x1 * lax.rsqrt(ms + EPS) * g2_ref[...]).astype(BF16)


def _out_proj(x2d, n, yp, attn, scale, w_out, g2):
    N, D = x2d.shape
    T = _tile(n, 512)
    hb = T // POOL_HALO
    last_hb = N // POOL_HALO - 1
    const = lambda *s: pl.BlockSpec(s, lambda i: (0,) * len(s))
    return pl.pallas_call(
        functools.partial(_out_proj_kernel, n),
        grid=(N // T,),
        in_specs=[
            pl.BlockSpec((T, D), lambda i: (i, 0)),
            pl.BlockSpec((T, POOL_WIDTH), lambda i: (i, 0)),
            pl.BlockSpec((POOL_HALO, POOL_WIDTH), lambda i: (jnp.maximum(i * hb - 1, 0), 0)),
            pl.BlockSpec((POOL_HALO, POOL_WIDTH), lambda i: (jnp.minimum((i + 1) * hb, last_hb), 0)),
            pl.BlockSpec((T, ATTN_WIDTH), lambda i: (i, 0)),
            const(1, POOL_WIDTH), const(*w_out.shape), const(1, D),
        ],
        out_specs=[
            pl.BlockSpec((T, D), lambda i: (i, 0)),
            pl.BlockSpec((T, D), lambda i: (i, 0)),
        ],
        out_shape=[
            jax.ShapeDtypeStruct((N, D), F32),
            jax.ShapeDtypeStruct((N, D), BF16),
        ],
        compiler_params=pltpu.CompilerParams(
            dimension_semantics=("arbitrary",), vmem_limit_bytes=VMEM_LIMIT),
        name="out_proj",
    )(x2d, yp, yp, yp, attn, scale, w_out, g2)


def _extract_top(vals, payload, k):
    R = vals.shape[0]
    rows = lax.broadcasted_iota(jnp.int32, vals.shape, 0)
    top_v, top_p = [], []
    for _ in range(k):
        m = jnp.max(vals, axis=0, keepdims=True)
        pos = jnp.min(jnp.where(vals == m, rows, R), axis=0, keepdims=True)
        sel = rows == pos
        top_v.append(m)
        if payload is None:
            top_p.append(pos)
        else:
            top_p.append(jnp.max(jnp.where(sel, payload, -1), axis=0, keepdims=True))
        vals = jnp.where(sel, -jnp.inf, vals)
    return jnp.concatenate(top_v, axis=0), jnp.concatenate(top_p, axis=0)


def _pair_candidates(v0, i0, v1, i1):
    K = PEER_TOPK
    sub = lax.broadcasted_iota(jnp.int32, (8, v0.shape[1]), 0)
    vals = [v0[0:1] + v1, v0[1:2] + v1[0:8]]
    eids = [i0[0:1] * PEER_NKEYS + i1, i0[1:2] * PEER_NKEYS + i1[0:8]]
    for a in range(2, 8):
        keep = sub < K // (a + 1)
        vals.append(jnp.where(keep, v0[a:a + 1] + v1[0:8], -jnp.inf))
        eids.append(i0[a:a + 1] * PEER_NKEYS + i1[0:8])
    vals.append(v0[8:16] + v1[0:1])
    eids.append(i0[8:16] * PEER_NKEYS + i1[0:1])
    return jnp.concatenate(vals, axis=0), jnp.concatenate(eids, axis=0)


def _peer_route_kernel(xn_ref, wqt_ref, sk_ref, e_ref, gt_ref, qt_scr, et_scr):
    T = xn_ref.shape[0]
    qt_scr[...] = lax.dot_general(wqt_ref[...], xn_ref[...], (((1,), (1,)), ((), ())),
                                  preferred_element_type=F32).astype(BF16)
    def head_body(h, carry):
        sub_v, sub_i = [], []
        for p in range(2):
            hp = h * 2 + p
            qhp = qt_scr[pl.ds(pl.multiple_of(hp * PEER_HALF, PEER_HALF), PEER_HALF), :]
            s = jnp.dot(sk_ref[hp], qhp, preferred_element_type=F32)
            tv, ti = _extract_top(s, None, PEER_TOPK)
            sub_v.append(tv)
            sub_i.append(ti)
        comb, eid = _pair_candidates(sub_v[0], sub_i[0], sub_v[1], sub_i[1])
        cv, ce = _extract_top(comb, eid, PEER_TOPK)
        ex = jnp.exp(cv - cv[0:1, :])
        gate = ex / jnp.sum(ex, axis=0, keepdims=True)
        row0 = pl.multiple_of(h * PEER_TOPK, PEER_TOPK)
        et_scr[pl.ds(row0, PEER_TOPK), :] = ce.astype(F32)
        gt_ref[0, pl.ds(row0, PEER_TOPK), :] = gate
        return carry

    lax.fori_loop(0, PEER_HEADS, head_body, 0)
    e_ref[...] = et_scr[...].T.astype(jnp.int32)


PEER_ROUTE_TILE = 128


def _peer_route(xn, wqt, sk):
    N, D = xn.shape
    T = PEER_ROUTE_TILE
    const = lambda *s: pl.BlockSpec(s, lambda i: (0,) * len(s))
    return pl.pallas_call(
        _peer_route_kernel,
        grid=(N // T,),
        in_specs=[pl.BlockSpec((T, D), lambda i: (i, 0)), const(*wqt.shape), const(*sk.shape)],
        out_specs=[
            pl.BlockSpec((T, PEER_PICKS), lambda i: (i, 0)),
            pl.BlockSpec((1, PEER_PICKS, T), lambda i: (i, 0, 0)),
        ],
        out_shape=[
            jax.ShapeDtypeStruct((N, PEER_PICKS), jnp.int32),
            jax.ShapeDtypeStruct((N // T, PEER_PICKS, T), F32),
        ],
        scratch_shapes=[
            pltpu.VMEM((wqt.shape[0], T), BF16),
            pltpu.VMEM((PEER_PICKS, T), F32),
        ],
        compiler_params=pltpu.CompilerParams(
            dimension_semantics=("arbitrary",), vmem_limit_bytes=VMEM_LIMIT),
        name="peer_route",
    )(xn, wqt, sk)


PEER_TOK_TILE = 16
PEER_PROLOGUE_UNROLL = 16
ROW_SUBLANES = 8
ISSUE_BEFORE_WAIT = 256
ISSUE_PER_GROUP = 3
ISSUE_PER_GATHER_STEP = 8
U_HALF_MASK = -65536


def _sum_sublanes_of_8(ps, sub):
    def comb(a, b, h):
        m = (sub & h) == 0
        if 2 * h == ROW_SUBLANES:
            return jnp.where(m, a, b) + pltpu.roll(jnp.where(m, b, a), h, 0)
        return (jnp.where(m, a, pltpu.roll(b, h, 0))
                + jnp.where(m, pltpu.roll(a, ROW_SUBLANES - h, 0), b))
    l1 = [comb(ps[2 * j], ps[2 * j + 1], 1) for j in range(4)]
    l2 = [comb(l1[0], l1[1], 2), comb(l1[2], l1[3], 2)]
    return comb(l2[0], l2[1], 4)


def _peer_ffn_kernel(ids_cur_ref, ids_nxt_ref, x1_ref, g2_ref, gt_ref, tab_ref, y_ref,
                     buf_a, buf_b, hs_scr, ab_scr, sem):
    TT = PEER_TOK_TILE
    rows = TT * PEER_PICKS
    D = ROW_SUBLANES * 128
    i = pl.program_id(0)
    sub = lax.broadcasted_iota(jnp.int32, (ROW_SUBLANES, 128), 0)
    lane_ids = lax.broadcasted_iota(jnp.int32, (PEER_PICKS, PEER_ROUTE_TILE), 1)
    g2 = g2_ref[...]

    def start_row(ids_ref, j_ids, dst, dst_sem, j, prio):
        pltpu.make_async_copy(tab_ref.at[ids_ref[0, 0, j_ids]], dst.at[j], dst_sem).start(priority=prio)

    def wait_tile(dst, dst_sem):
        pltpu.make_async_copy(dst, dst, dst_sem).wait()

    def process(cur, cur_sem, nxt, nxt_sem, nxt_ids_ref, nxt_ids_off, tok0):
        pending = iter(range(rows))

        def issue(n):
            for _ in range(n):
                k = next(pending, None)
                if k is not None:
                    start_row(nxt_ids_ref, nxt_ids_off + k, nxt, nxt_sem, k, k % 2)

        issue(ISSUE_BEFORE_WAIT)
        wait_tile(cur, cur_sem)
        lanes = lane_ids - ((i * 2 * TT) % PEER_ROUTE_TILE + tok0)

        for t in range(TT):
            base = t * PEER_PICKS
            x1 = x1_ref[tok0 + t]
            ms = jnp.sum(x1 * x1, axis=(0, 1), keepdims=True) * (1.0 / D)
            xt = x1 * lax.rsqrt(ms + EPS) * g2
            groups = []
            for g in range(PEER_PICKS // ROW_SUBLANES):
                issue(ISSUE_PER_GROUP)
                ps = []
                for j in range(ROW_SUBLANES):
                    w = cur[base + g * ROW_SUBLANES + j]
                    ps.append(lax.bitcast_convert_type(w & jnp.int32(U_HALF_MASK), F32) * xt)
                groups.append(_sum_sublanes_of_8(ps, sub))
            hs_scr[t] = jnp.concatenate(groups, axis=0)

        hid = jnp.zeros((PEER_PICKS, PEER_ROUTE_TILE), F32)
        for t in range(TT):
            issue(ISSUE_PER_GATHER_STEP)
            hid = jnp.where(lanes == t, jnp.sum(hs_scr[t], axis=-1, keepdims=True), hid)
        act = 0.5 * hid * (1.0 + lax.erf(hid * (1.0 / math.sqrt(2.0)))) * gt_ref[0]
        for t in range(TT):
            issue(ISSUE_PER_GATHER_STEP)
            a_t = jnp.sum(jnp.where(lanes == t, act, 0.0), axis=-1, keepdims=True)
            ab_scr[t] = jnp.broadcast_to(a_t, ab_scr.shape[1:])

        for t in range(TT):
            base = t * PEER_PICKS
            accs = [jnp.zeros((ROW_SUBLANES, 128), F32) for _ in range(4)]
            for k in range(PEER_PICKS):
                if k % ROW_SUBLANES < ISSUE_PER_GROUP:
                    issue(1)
                gv = lax.bitcast_convert_type(cur[base + k] << 16, F32)
                accs[k % 4] = accs[k % 4] + gv * jnp.broadcast_to(ab_scr[t, k:k + 1, :], gv.shape)
            y_ref[tok0 + t] = x1_ref[tok0 + t] + ((accs[0] + accs[1]) + (accs[2] + accs[3]))
        issue(rows)

    @pl.when(i == 0)
    def _():
        def body(c, carry):
            for r in range(PEER_PROLOGUE_UNROLL):
                j = c * PEER_PROLOGUE_UNROLL + r
                start_row(ids_cur_ref, j, buf_a, sem.at[0], j, r % 2)
            return carry
        lax.fori_loop(0, rows // PEER_PROLOGUE_UNROLL, body, 0)

    process(buf_a, sem.at[0], buf_b, sem.at[1], ids_cur_ref, rows, 0)
    process(buf_b, sem.at[1], buf_a, sem.at[0], ids_nxt_ref, 0, TT)

    @pl.when(i == pl.num_programs(0) - 1)
    def _():
        wait_tile(buf_a, sem.at[0])


def _peer_ffn(ids, gt, x1_rows, g2_rows, table):
    N = x1_rows.shape[0]
    TT = PEER_TOK_TILE
    ns = N // (2 * TT)
    rows = TT * PEER_PICKS
    ids3 = ids.reshape(ns, 1, 2 * rows)
    tok = pl.BlockSpec((2 * TT, ROW_SUBLANES, 128), lambda i: (i, 0, 0))
    return pl.pallas_call(
        _peer_ffn_kernel,
        grid=(ns,),
        in_specs=[
            pl.BlockSpec((1, 1, 2 * rows), lambda i: (i, 0, 0), memory_space=pltpu.SMEM),
            pl.BlockSpec((1, 1, 2 * rows), lambda i: (jnp.minimum(i + 1, ns - 1), 0, 0),
                         memory_space=pltpu.SMEM),
            tok,
            pl.BlockSpec((ROW_SUBLANES, 128), lambda i: (0, 0)),
            pl.BlockSpec((1, PEER_PICKS, PEER_ROUTE_TILE),
                         lambda i: (i * 2 * TT // PEER_ROUTE_TILE, 0, 0)),
            pl.BlockSpec(memory_space=pl.ANY),
        ],
        out_specs=tok,
        out_shape=jax.ShapeDtypeStruct(x1_rows.shape, F32),
        scratch_shapes=[
            pltpu.VMEM((rows, ROW_SUBLANES, 128), jnp.int32),
            pltpu.VMEM((rows, ROW_SUBLANES, 128), jnp.int32),
            pltpu.VMEM((TT, PEER_PICKS, 128), F32),
            pltpu.VMEM((TT, PEER_PICKS, 128), F32),
            pltpu.SemaphoreType.DMA((2,)),
        ],
        compiler_params=pltpu.CompilerParams(
            dimension_semantics=("arbitrary",), vmem_limit_bytes=VMEM_LIMIT),
        name="peer_ffn",
    )(ids3, ids3, x1_rows, g2_rows, gt, table)


def _rope_partner(width):
    c = np.arange(width)
    j = c % ROPE_HALF
    return np.where(j < ROPE_QUARTER, c + ROPE_QUARTER, c - ROPE_QUARTER)


def _rope_tables(n):
    rows = n // GRID_W
    row = jnp.repeat(jnp.arange(rows, dtype=F32), GRID_W)
    col = jnp.tile(jnp.arange(GRID_W, dtype=F32), rows)
    inv = 1.0 / (ROPE_THETA ** (jnp.arange(0, ROPE_HALF, 2, dtype=F32) / ROPE_HALF))
    ang_r = row[:, None] * inv
    ang_c = col[:, None] * inv
    cos = jnp.concatenate([jnp.cos(ang_r)] * 2 + [jnp.cos(ang_c)] * 2, axis=-1)
    sin = jnp.concatenate([-jnp.sin(ang_r), jnp.sin(ang_r), -jnp.sin(ang_c), jnp.sin(ang_c)], axis=-1)
    return jnp.tile(cos, (1, N_Q_HEADS)), jnp.tile(sin, (1, N_Q_HEADS))


def _layer(x, p):
    B, n, D = x.shape
    N = B * n
    x2d = x.reshape(N, D)
    cos_t, sin_t = _rope_tables(n)
    yp, q, kt, v = _in_proj(x2d, n, p["g1"], p["w_ext"], p["pw"], p["gq"], p["gqs"], p["gk"],
                            p["gks"], cos_t, sin_t, p["mavg"])
    attn = _attention(q.reshape(B, n, ATTN_WIDTH), kt, v.reshape(B, n, KV_WIDTH))
    x1, xn = _out_proj(x2d, n, yp, attn.reshape(N, ATTN_WIDTH), p["scale"], p["w_out"], p["g2"])
    ids, gt = _peer_route(xn, p["wqt"], p["sk"])
    y = _peer_ffn(ids, gt, x1.reshape(N, ROW_SUBLANES, 128), p["g2"].reshape(ROW_SUBLANES, 128),
                  p["table"])
    return y.reshape(B, n, D)


def _prepare(norm1_g, w_in, pool_w, pool_scale, q_norm_g, k_norm_g, w_out, norm2_g,
             peer_wq, peer_subkeys, peer_u, peer_v):
    pq = _rope_partner(ATTN_WIDTH)
    pk = _rope_partner(KV_WIDTH)
    o_q = POOL_WIDTH
    o_k = o_q + ATTN_WIDTH
    w_ext = jnp.concatenate([w_in, w_in[:, o_q + pq], w_in[:, o_k + pk]], axis=1).astype(BF16)
    gq = jnp.tile(q_norm_g, N_Q_HEADS)
    gk = jnp.tile(k_norm_g, N_KV_HEADS)
    blk = np.arange(ATTN_WIDTH) // HEAD_DIM
    mavg = jnp.asarray((blk[:, None] == blk[None, :]) / HEAD_DIM, BF16)
    ub = lax.bitcast_convert_type(peer_u.astype(BF16), jnp.uint16).astype(jnp.uint32)
    vb = lax.bitcast_convert_type(peer_v.astype(BF16), jnp.uint16).astype(jnp.uint32)
    table = lax.bitcast_convert_type((ub << 16) | vb, jnp.int32).reshape(-1, ROW_SUBLANES, 128)
    return dict(
        g1=norm1_g[None, :], w_ext=w_ext, pw=pool_w.astype(BF16),
        gq=gq[None, :], gqs=gq[pq][None, :], gk=gk[None, :], gks=gk[pk][None, :], mavg=mavg,
        scale=pool_scale[None, :], w_out=w_out.astype(BF16), g2=norm2_g[None, :],
        wqt=peer_wq.T.astype(BF16),
        sk=peer_subkeys.reshape(PEER_HEADS * 2, PEER_NKEYS, PEER_HALF).astype(BF16),
        table=table,
    )


def kernel(x_prompt, x_sample, norm1_g, w_in, pool_w, pool_scale, q_norm_g, k_norm_g, w_out,
           norm2_g, peer_wq, peer_subkeys, peer_u, peer_v):
    y_prompt, y_sample = x_prompt, x_sample
    for l in range(norm1_g.shape[0]):
        p = _prepare(norm1_g[l], w_in[l], pool_w[l], pool_scale[l], q_norm_g[l], k_norm_g[l],
                     w_out[l], norm2_g[l], peer_wq[l], peer_subkeys[l], peer_u[l], peer_v[l])
        y_prompt = _layer(y_prompt, p)
        y_sample = _layer(y_sample, p)
    return (y_prompt, y_sample)
```

```python
import functools
import math

import numpy as np
import jax
import jax.numpy as jnp
from jax import lax
from jax.experimental import pallas as pl
from jax.experimental.pallas import tpu as pltpu

F32 = jnp.float32
BF16 = jnp.bfloat16

EPS = 1e-6
GRID_W = 64
POOL_WINDOWS = (2, 4, 8, 16)
POOL_GROUP_DIM = 128
POOL_WIDTH = 512
POOL_HALO = 8
HEAD_DIM = 64
N_Q_HEADS = 8
N_KV_HEADS = 2
GQA_GROUP = N_Q_HEADS // N_KV_HEADS
ATTN_WIDTH = N_Q_HEADS * HEAD_DIM
KV_WIDTH = N_KV_HEADS * HEAD_DIM
ROPE_HALF = HEAD_DIM // 2
ROPE_QUARTER = ROPE_HALF // 2
ROPE_THETA = 10000.0
ATTN_SCALE = 1.0 / math.sqrt(HEAD_DIM)
PEER_HEADS = 8
PEER_NKEYS = 128
PEER_HALF = 128
PEER_TOPK = 16
PEER_PICKS = PEER_HEADS * PEER_TOPK

VMEM_LIMIT = 48 * 1024 * 1024


def _tile(n, want):
    t = min(n, want)
    assert n % t == 0, (n, t)
    return t


def _head_mean_square(a, mavg):
    sq = a * a
    hi = sq.astype(BF16)
    lo = (sq - hi.astype(F32)).astype(BF16)
    return (jnp.dot(hi, mavg, preferred_element_type=F32)
            + jnp.dot(lo, mavg, preferred_element_type=F32))


def _in_proj_kernel(x_ref, g1_ref, w_ref, pw_ref, gq_ref, gqs_ref, gk_ref, gks_ref,
                    cos_ref, sin_ref, mavg_ref, yp_ref, q_ref, kt_ref, v_ref):
    x = x_ref[...]
    ms = jnp.mean(x * x, axis=-1, keepdims=True)
    h = (x * lax.rsqrt(ms + EPS) * g1_ref[...]).astype(BF16)
    z = jnp.dot(h, w_ref[...], preferred_element_type=F32)
    for g in range(len(POOL_WINDOWS)):
        sl = slice(g * POOL_GROUP_DIM, (g + 1) * POOL_GROUP_DIM)
        yp_ref[:, sl] = jnp.dot(z[:, sl].astype(BF16), pw_ref[g], preferred_element_type=F32)
    o_q = POOL_WIDTH
    o_k = o_q + ATTN_WIDTH
    o_v = o_k + KV_WIDTH
    o_qs = o_v + KV_WIDTH
    o_ks = o_qs + ATTN_WIDTH
    zq, zk, zv = z[:, o_q:o_k], z[:, o_k:o_v], z[:, o_v:o_qs]
    zqs, zks = z[:, o_qs:o_ks], z[:, o_ks:o_ks + KV_WIDTH]
    cos = cos_ref[...]
    sin = sin_ref[...]
    rq = lax.rsqrt(_head_mean_square(zq, mavg_ref[...]) + EPS)
    q = ((zq * rq * gq_ref[...]) * cos + (zqs * rq * gqs_ref[...]) * sin) * ATTN_SCALE
    q_ref[...] = q.astype(BF16)
    rk = lax.rsqrt(_head_mean_square(zk, mavg_ref[:KV_WIDTH, :KV_WIDTH]) + EPS)
    k = (zk * rk * gk_ref[...]) * cos[:, :KV_WIDTH] + (zks * rk * gks_ref[...]) * sin[:, :KV_WIDTH]
    kt_ref[0] = k.T.astype(BF16)
    v_ref[...] = zv.astype(BF16)


def _in_proj(x2d, n, g1, w_ext, pw, gq, gqs, gk, gks, cos_t, sin_t, mavg):
    N, D = x2d.shape
    T = _tile(n, 512)
    tiles_per_seq = n // T
    const = lambda *s: pl.BlockSpec(s, lambda i: (0,) * len(s))
    return pl.pallas_call(
        _in_proj_kernel,
        grid=(N // T,),
        in_specs=[
            pl.BlockSpec((T, D), lambda i: (i, 0)),
            const(1, D), const(*w_ext.shape), const(*pw.shape),
            const(1, ATTN_WIDTH), const(1, ATTN_WIDTH), const(1, KV_WIDTH), const(1, KV_WIDTH),
            pl.BlockSpec((T, ATTN_WIDTH), lambda i: (i % tiles_per_seq, 0)),
            pl.BlockSpec((T, ATTN_WIDTH), lambda i: (i % tiles_per_seq, 0)),
            const(ATTN_WIDTH, ATTN_WIDTH),
        ],
        out_specs=[
            pl.BlockSpec((T, POOL_WIDTH), lambda i: (i, 0)),
            pl.BlockSpec((T, ATTN_WIDTH), lambda i: (i, 0)),
            pl.BlockSpec((1, KV_WIDTH, T), lambda i: (i // tiles_per_seq, 0, i % tiles_per_seq)),
            pl.BlockSpec((T, KV_WIDTH), lambda i: (i, 0)),
        ],
        out_shape=[
            jax.ShapeDtypeStruct((N, POOL_WIDTH), F32),
            jax.ShapeDtypeStruct((N, ATTN_WIDTH), BF16),
            jax.ShapeDtypeStruct((N // n, KV_WIDTH, n), BF16),
            jax.ShapeDtypeStruct((N, KV_WIDTH), BF16),
        ],
        compiler_params=pltpu.CompilerParams(
            dimension_semantics=("arbitrary",), vmem_limit_bytes=VMEM_LIMIT),
        name="in_proj",
    )(x2d, g1, w_ext, pw, gq, gqs, gk, gks, cos_t, sin_t, mavg)


def _attn_kernel(q_ref, kt_ref, v_ref, o_ref):
    outs = []
    for j in range(N_KV_HEADS):
        kt = kt_ref[0, j * HEAD_DIM:(j + 1) * HEAD_DIM, :]
        v = v_ref[0, :, j * HEAD_DIM:(j + 1) * HEAD_DIM]
        for g in range(GQA_GROUP):
            hq = j * GQA_GROUP + g
            q = q_ref[0, :, hq * HEAD_DIM:(hq + 1) * HEAD_DIM]
            s = jnp.dot(q, kt, preferred_element_type=F32)
            m = jnp.max(s, axis=-1, keepdims=True)
            p = jnp.exp(s - m)
            l = jnp.sum(p, axis=-1, keepdims=True)
            o = jnp.dot(p.astype(BF16), v, preferred_element_type=F32)
            outs.append(o / l)
    o_ref[0] = jnp.concatenate(outs, axis=-1).astype(BF16)


def _attention(q, kt, v):
    B, n, _ = q.shape
    tq = _tile(n, 256)
    return pl.pallas_call(
        _attn_kernel,
        grid=(B, n // tq),
        in_specs=[
            pl.BlockSpec((1, tq, ATTN_WIDTH), lambda b, i: (b, i, 0)),
            pl.BlockSpec((1, KV_WIDTH, n), lambda b, i: (b, 0, 0)),
            pl.BlockSpec((1, n, KV_WIDTH), lambda b, i: (b, 0, 0)),
        ],
        out_specs=pl.BlockSpec((1, tq, ATTN_WIDTH), lambda b, i: (b, i, 0)),
        out_shape=jax.ShapeDtypeStruct((B, n, ATTN_WIDTH), BF16),
        compiler_params=pltpu.CompilerParams(
            dimension_semantics=("arbitrary", "arbitrary"), vmem_limit_bytes=VMEM_LIMIT),
        name="attention",
    )(q, kt, v)


def _out_proj_kernel(n, x_ref, yp_ref, prev_ref, next_ref, at_ref, sc_ref, w_ref, g2_ref,
                     x1_ref, xn_ref):
    T = x_ref.shape[0]
    tiles_per_seq = n // T
    si = pl.program_id(0) % tiles_per_seq
    not_first = (si > 0).astype(F32)
    not_last = (si < tiles_per_seq - 1).astype(F32)
    cur = yp_ref[...]
    ext = jnp.concatenate([prev_ref[...] * not_first, cur, next_ref[...] * not_last], axis=0)
    t = si * T + lax.broadcasted_iota(jnp.int32, (T, 1), 0)
    pooled = []
    for g, w in enumerate(POOL_WINDOWS):
        sl = slice(g * POOL_GROUP_DIM, (g + 1) * POOL_GROUP_DIM)
        eg = ext[:, sl]
        acc = eg[POOL_HALO - w // 2:POOL_HALO - w // 2 + T]
        for d in range(-w // 2 + 1, w - w // 2):
            acc = acc + eg[POOL_HALO + d:POOL_HALO + d + T]
        cnt = jnp.minimum(t + (w - w // 2), n) - jnp.maximum(t - w // 2, 0)
        pooled.append(acc / cnt.astype(F32) - cur[:, sl])
    pool = jnp.concatenate(pooled, axis=-1) * sc_ref[...]
    mix = jnp.concatenate([pool.astype(BF16), at_ref[...]], axis=-1)
    x1 = x_ref[...] + jnp.dot(mix, w_ref[...], preferred_element_type=F32)
    x1_ref[...] = x1
    ms = jnp.mean(x1 * x1, axis=-1, keepdims=True)
    xn_ref[...] = (x1 * lax.rsqrt(ms + EPS) * g2_ref[...]).astype(BF16)


def _out_proj(x2d, n, yp, attn, scale, w_out, g2):
    N, D = x2d.shape
    T = _tile(n, 512)
    hb = T // POOL_HALO
    last_hb = N // POOL_HALO - 1
    const = lambda *s: pl.BlockSpec(s, lambda i: (0,) * len(s))
    return pl.pallas_call(
        functools.partial(_out_proj_kernel, n),
        grid=(N // T,),
        in_specs=[
            pl.BlockSpec((T, D), lambda i: (i, 0)),
            pl.BlockSpec((T, POOL_WIDTH), lambda i: (i, 0)),
            pl.BlockSpec((POOL_HALO, POOL_WIDTH), lambda i: (jnp.maximum(i * hb - 1, 0), 0)),
            pl.BlockSpec((POOL_HALO, POOL_WIDTH), lambda i: (jnp.minimum((i + 1) * hb, last_hb), 0)),
            pl.BlockSpec((T, ATTN_WIDTH), lambda i: (i, 0)),
            const(1, POOL_WIDTH), const(*w_out.shape), const(1, D),
        ],
        out_specs=[
            pl.BlockSpec((T, D), lambda i: (i, 0)),
            pl.BlockSpec((T, D), lambda i: (i, 0)),
        ],
        out_shape=[
            jax.ShapeDtypeStruct((N, D), F32),
            jax.ShapeDtypeStruct((N, D), BF16),
        ],
        compiler_params=pltpu.CompilerParams(
            dimension_semantics=("arbitrary",), vmem_limit_bytes=VMEM_LIMIT),
        name="out_proj",
    )(x2d, yp, yp, yp, attn, scale, w_out, g2)


def _extract_top(vals, payload, k):
    R = vals.shape[0]
    rows = lax.broadcasted_iota(jnp.int32, vals.shape, 0)
    top_v, top_p = [], []
    for _ in range(k):
        m = jnp.max(vals, axis=0, keepdims=True)
        pos = jnp.min(jnp.where(vals == m, rows, R), axis=0, keepdims=True)
        sel = rows == pos
        top_v.append(m)
        if payload is None:
            top_p.append(pos)
        else:
            top_p.append(jnp.max(jnp.where(sel, payload, -1), axis=0, keepdims=True))
        vals = jnp.where(sel, -jnp.inf, vals)
    return jnp.concatenate(top_v, axis=0), jnp.concatenate(top_p, axis=0)


def _pair_candidates(v0, i0, v1, i1):
    K = PEER_TOPK
    sub = lax.broadcasted_iota(jnp.int32, (8, v0.shape[1]), 0)
    vals = [v0[0:1] + v1, v0[1:2] + v1[0:8]]
    eids = [i0[0:1] * PEER_NKEYS + i1, i0[1:2] * PEER_NKEYS + i1[0:8]]
    for a in range(2, 8):
        keep = sub < K // (a + 1)
        vals.append(jnp.where(keep, v0[a:a + 1] + v1[0:8], -jnp.inf))
        eids.append(i0[a:a + 1] * PEER_NKEYS + i1[0:8])
    vals.append(v0[8:16] + v1[0:1])
    eids.append(i0[8:16] * PEER_NKEYS + i1[0:1])
    return jnp.concatenate(vals, axis=0), jnp.concatenate(eids, axis=0)


def _peer_route_kernel(xn_ref, wqt_ref, sk_ref, e_ref, gt_ref, qt_scr, et_scr):
    T = xn_ref.shape[0]
    qt_scr[...] = lax.dot_general(wqt_ref[...], xn_ref[...], (((1,), (1,)), ((), ())),
                                  preferred_element_type=F32).astype(BF16)
    def route_head(h):
        sub_v, sub_i = [], []
        for p in range(2):
            hp = h * 2 + p
            qhp = qt_scr[pl.ds(pl.multiple_of(hp * PEER_HALF, PEER_HALF), PEER_HALF), :]
            s = jnp.dot(sk_ref[hp], qhp, preferred_element_type=F32)
            tv, ti = _extract_top(s, None, PEER_TOPK)
            sub_v.append(tv)
            sub_i.append(ti)
        comb, eid = _pair_candidates(sub_v[0], sub_i[0], sub_v[1], sub_i[1])
        cv, ce = _extract_top(comb, eid, PEER_TOPK)
        ex = jnp.exp(cv - cv[0:1, :])
        gate = ex / jnp.sum(ex, axis=0, keepdims=True)
        row0 = pl.multiple_of(h * PEER_TOPK, PEER_TOPK)
        et_scr[pl.ds(row0, PEER_TOPK), :] = ce.astype(F32)
        gt_ref[0, pl.ds(row0, PEER_TOPK), :] = gate

    def heads_body(c, carry):
        for r in range(ROUTE_HEADS_PER_ITER):
            route_head(c * ROUTE_HEADS_PER_ITER + r)
        return carry

    lax.fori_loop(0, PEER_HEADS // ROUTE_HEADS_PER_ITER, heads_body, 0)
    e_ref[...] = et_scr[...].T.astype(jnp.int32)


PEER_ROUTE_TILE = 128
ROUTE_HEADS_PER_ITER = 4


def _peer_route(xn, wqt, sk):
    N, D = xn.shape
    T = PEER_ROUTE_TILE
    const = lambda *s: pl.BlockSpec(s, lambda i: (0,) * len(s))
    return pl.pallas_call(
        _peer_route_kernel,
        grid=(N // T,),
        in_specs=[pl.BlockSpec((T, D), lambda i: (i, 0)), const(*wqt.shape), const(*sk.shape)],
        out_specs=[
            pl.BlockSpec((T, PEER_PICKS), lambda i: (i, 0)),
            pl.BlockSpec((1, PEER_PICKS, T), lambda i: (i, 0, 0)),
        ],
        out_shape=[
            jax.ShapeDtypeStruct((N, PEER_PICKS), jnp.int32),
            jax.ShapeDtypeStruct((N // T, PEER_PICKS, T), F32),
        ],
        scratch_shapes=[
            pltpu.VMEM((wqt.shape[0], T), BF16),
            pltpu.VMEM((PEER_PICKS, T), F32),
        ],
        compiler_params=pltpu.CompilerParams(
            dimension_semantics=("arbitrary",), vmem_limit_bytes=VMEM_LIMIT),
        name="peer_route",
    )(xn, wqt, sk)


PEER_TOK_TILE = 16
PEER_PROLOGUE_UNROLL = 16
ROW_SUBLANES = 8
ISSUE_BEFORE_WAIT = 512
ISSUE_PER_GROUP = 3
ISSUE_PER_GATHER_STEP = 8
U_HALF_MASK = -65536


def _sum_sublanes_of_8(ps, sub):
    def comb(a, b, h):
        m = (sub & h) == 0
        if 2 * h == ROW_SUBLANES:
            return jnp.where(m, a, b) + pltpu.roll(jnp.where(m, b, a), h, 0)
        return (jnp.where(m, a, pltpu.roll(b, h, 0))
                + jnp.where(m, pltpu.roll(a, ROW_SUBLANES - h, 0), b))
    l1 = [comb(ps[2 * j], ps[2 * j + 1], 1) for j in range(4)]
    l2 = [comb(l1[0], l1[1], 2), comb(l1[2], l1[3], 2)]
    return comb(l2[0], l2[1], 4)


def _peer_ffn_kernel(ids_cur_ref, ids_nxt_ref, x1_ref, g2_ref, gt_ref, tab_ref, y_ref,
                     buf_a, buf_b, hs_scr, ab_scr, sem):
    TT = PEER_TOK_TILE
    rows = TT * PEER_PICKS
    D = ROW_SUBLANES * 128
    i = pl.program_id(0)
    sub = lax.broadcasted_iota(jnp.int32, (ROW_SUBLANES, 128), 0)
    lane_ids = lax.broadcasted_iota(jnp.int32, (PEER_PICKS, PEER_ROUTE_TILE), 1)
    g2 = g2_ref[...]

    def start_row(ids_ref, j_ids, dst, dst_sem, j, prio):
        pltpu.make_async_copy(tab_ref.at[ids_ref[0, 0, j_ids]], dst.at[j], dst_sem).start(priority=prio)

    def wait_tile(dst, dst_sem):
        pltpu.make_async_copy(dst, dst, dst_sem).wait()

    def process(cur, cur_sem, nxt, nxt_sem, nxt_ids_ref, nxt_ids_off, tok0):
        pending = iter(range(rows))

        def issue(n):
            for _ in range(n):
                k = next(pending, None)
                if k is not None:
                    start_row(nxt_ids_ref, nxt_ids_off + k, nxt, nxt_sem, k, k % 2)

        issue(ISSUE_BEFORE_WAIT)
        wait_tile(cur, cur_sem)
        lanes = lane_ids - ((i * 2 * TT) % PEER_ROUTE_TILE + tok0)

        for t in range(TT):
            base = t * PEER_PICKS
            x1 = x1_ref[tok0 + t]
            ms = jnp.sum(x1 * x1, axis=(0, 1), keepdims=True) * (1.0 / D)
            xt = x1 * lax.rsqrt(ms + EPS) * g2
            groups = []
            for g in range(PEER_PICKS // ROW_SUBLANES):
                issue(ISSUE_PER_GROUP)
                ps = []
                for j in range(ROW_SUBLANES):
                    w = cur[base + g * ROW_SUBLANES + j]
                    ps.append(lax.bitcast_convert_type(w & jnp.int32(U_HALF_MASK), F32) * xt)
                groups.append(_sum_sublanes_of_8(ps, sub))
            hs_scr[t] = jnp.concatenate(groups, axis=0)

        hid = jnp.zeros((PEER_PICKS, PEER_ROUTE_TILE), F32)
        for t in range(TT):
            issue(ISSUE_PER_GATHER_STEP)
            hid = jnp.where(lanes == t, jnp.sum(hs_scr[t], axis=-1, keepdims=True), hid)
        act = 0.5 * hid * (1.0 + lax.erf(hid * (1.0 / math.sqrt(2.0)))) * gt_ref[0]
        for t in range(TT):
            issue(ISSUE_PER_GATHER_STEP)
            a_t = jnp.sum(jnp.where(lanes == t, act, 0.0), axis=-1, keepdims=True)
            ab_scr[t] = jnp.broadcast_to(a_t, ab_scr.shape[1:])

        for t in range(TT):
            base = t * PEER_PICKS
            accs = [jnp.zeros((ROW_SUBLANES, 128), F32) for _ in range(4)]
            for k in range(PEER_PICKS):
                if k % ROW_SUBLANES < ISSUE_PER_GROUP:
                    issue(1)
                gv = lax.bitcast_convert_type(cur[base + k] << 16, F32)
                accs[k % 4] = accs[k % 4] + gv * jnp.broadcast_to(ab_scr[t, k:k + 1, :], gv.shape)
            y_ref[tok0 + t] = x1_ref[tok0 + t] + ((accs[0] + accs[1]) + (accs[2] + accs[3]))
        issue(rows)

    @pl.when(i == 0)
    def _():
        def body(c, carry):
            for r in range(PEER_PROLOGUE_UNROLL):
                j = c * PEER_PROLOGUE_UNROLL + r
                start_row(ids_cur_ref, j, buf_a, sem.at[0], j, r % 2)
            return carry
        lax.fori_loop(0, rows // PEER_PROLOGUE_UNROLL, body, 0)

    process(buf_a, sem.at[0], buf_b, sem.at[1], ids_cur_ref, rows, 0)
    process(buf_b, sem.at[1], buf_a, sem.at[0], ids_nxt_ref, 0, TT)

    @pl.when(i == pl.num_programs(0) - 1)
    def _():
        wait_tile(buf_a, sem.at[0])


def _peer_ffn(ids, gt, x1_rows, g2_rows, table):
    N = x1_rows.shape[0]
    TT = PEER_TOK_TILE
    ns = N // (2 * TT)
    rows = TT * PEER_PICKS
    ids3 = ids.reshape(ns, 1, 2 * rows)
    tok = pl.BlockSpec((2 * TT, ROW_SUBLANES, 128), lambda i: (i, 0, 0))
    return pl.pallas_call(
        _peer_ffn_kernel,
        grid=(ns,),
        in_specs=[
            pl.BlockSpec((1, 1, 2 * rows), lambda i: (i, 0, 0), memory_space=pltpu.SMEM),
            pl.BlockSpec((1, 1, 2 * rows), lambda i: (jnp.minimum(i + 1, ns - 1), 0, 0),
                         memory_space=pltpu.SMEM),
            tok,
            pl.BlockSpec((ROW_SUBLANES, 128), lambda i: (0, 0)),
            pl.BlockSpec((1, PEER_PICKS, PEER_ROUTE_TILE),
                         lambda i: (i * 2 * TT // PEER_ROUTE_TILE, 0, 0)),
            pl.BlockSpec(memory_space=pl.ANY),
        ],
        out_specs=tok,
        out_shape=jax.ShapeDtypeStruct(x1_rows.shape, F32),
        scratch_shapes=[
            pltpu.VMEM((rows, ROW_SUBLANES, 128), jnp.int32),
            pltpu.VMEM((rows, ROW_SUBLANES, 128), jnp.int32),
            pltpu.VMEM((TT, PEER_PICKS, 128), F32),
            pltpu.VMEM((TT, PEER_PICKS, 128), F32),
            pltpu.SemaphoreType.DMA((2,)),
        ],
        compiler_params=pltpu.CompilerParams(
            dimension_semantics=("arbitrary",), vmem_limit_bytes=VMEM_LIMIT),
        name="peer_ffn",
    )(ids3, ids3, x1_rows, g2_rows, gt, table)


def _rope_partner(width):
    c = np.arange(width)
    j = c % ROPE_HALF
    return np.where(j < ROPE_QUARTER, c + ROPE_QUARTER, c - ROPE_QUARTER)


def _rope_tables(n):
    rows = n // GRID_W
    row = jnp.repeat(jnp.arange(rows, dtype=F32), GRID_W)
    col = jnp.tile(jnp.arange(GRID_W, dtype=F32), rows)
    inv = 1.0 / (ROPE_THETA ** (jnp.arange(0, ROPE_HALF, 2, dtype=F32) / ROPE_HALF))
    ang_r = row[:, None] * inv
    ang_c = col[:, None] * inv
    cos = jnp.concatenate([jnp.cos(ang_r)] * 2 + [jnp.cos(ang_c)] * 2, axis=-1)
    sin = jnp.concatenate([-jnp.sin(ang_r), jnp.sin(ang_r), -jnp.sin(ang_c), jnp.sin(ang_c)], axis=-1)
    return jnp.tile(cos, (1, N_Q_HEADS)), jnp.tile(sin, (1, N_Q_HEADS))


def _layer(x, p):
    B, n, D = x.shape
    N = B * n
    x2d = x.reshape(N, D)
    cos_t, sin_t = _rope_tables(n)
    yp, q, kt, v = _in_proj(x2d, n, p["g1"], p["w_ext"], p["pw"], p["gq"], p["gqs"], p["gk"],
                            p["gks"], cos_t, sin_t, p["mavg"])
    attn = _attention(q.reshape(B, n, ATTN_WIDTH), kt, v.reshape(B, n, KV_WIDTH))
    x1, xn = _out_proj(x2d, n, yp, attn.reshape(N, ATTN_WIDTH), p["scale"], p["w_out"], p["g2"])
    ids, gt = _peer_route(xn, p["wqt"], p["sk"])
    y = _peer_ffn(ids, gt, x1.reshape(N, ROW_SUBLANES, 128), p["g2"].reshape(ROW_SUBLANES, 128),
                  p["table"])
    return y.reshape(B, n, D)


def _prepare(norm1_g, w_in, pool_w, pool_scale, q_norm_g, k_norm_g, w_out, norm2_g,
             peer_wq, peer_subkeys, peer_u, peer_v):
    pq = _rope_partner(ATTN_WIDTH)
    pk = _rope_partner(KV_WIDTH)
    o_q = POOL_WIDTH
    o_k = o_q + ATTN_WIDTH
    w_ext = jnp.concatenate([w_in, w_in[:, o_q + pq], w_in[:, o_k + pk]], axis=1).astype(BF16)
    gq = jnp.tile(q_norm_g, N_Q_HEADS)
    gk = jnp.tile(k_norm_g, N_KV_HEADS)
    blk = np.arange(ATTN_WIDTH) // HEAD_DIM
    mavg = jnp.asarray((blk[:, None] == blk[None, :]) / HEAD_DIM, BF16)
    ub = lax.bitcast_convert_type(peer_u.astype(BF16), jnp.uint16).astype(jnp.uint32)
    vb = lax.bitcast_convert_type(peer_v.astype(BF16), jnp.uint16).astype(jnp.uint32)
    table = lax.bitcast_convert_type((ub << 16) | vb, jnp.int32).reshape(-1, ROW_SUBLANES, 128)
    return dict(
        g1=norm1_g[None, :], w_ext=w_ext, pw=pool_w.astype(BF16),
        gq=gq[None, :], gqs=gq[pq][None, :], gk=gk[None, :], gks=gk[pk][None, :], mavg=mavg,
        scale=pool_scale[None, :], w_out=w_out.astype(BF16), g2=norm2_g[None, :],
        wqt=peer_wq.T.astype(BF16),
        sk=peer_subkeys.reshape(PEER_HEADS * 2, PEER_NKEYS, PEER_HALF).astype(BF16),
        table=table,
    )


def kernel(x_prompt, x_sample, norm1_g, w_in, pool_w, pool_scale, q_norm_g, k_norm_g, w_out,
           norm2_g, peer_wq, peer_subkeys, peer_u, peer_v):
    y_prompt, y_sample = x_prompt, x_sample
    for l in range(norm1_g.shape[0]):
        p = _prepare(norm1_g[l], w_in[l], pool_w[l], pool_scale[l], q_norm_g[l], k_norm_g[l],
                     w_out[l], norm2_g[l], peer_wq[l], peer_subkeys[l], peer_u[l], peer_v[l])
        y_prompt = _layer(y_prompt, p)
        y_sample = _layer(y_sample, p)
    return (y_prompt, y_sample)
```

```python
import functools
import math

import numpy as np
import jax
import jax.numpy as jnp
from jax import lax
from jax.experimental import pallas as pl
from jax.experimental.pallas import tpu as pltpu

F32 = jnp.float32
BF16 = jnp.bfloat16

EPS = 1e-6
GRID_W = 64
POOL_WINDOWS = (2, 4, 8, 16)
POOL_GROUP_DIM = 128
POOL_WIDTH = 512
POOL_HALO = 8
HEAD_DIM = 64
N_Q_HEADS = 8
N_KV_HEADS = 2
GQA_GROUP = N_Q_HEADS // N_KV_HEADS
ATTN_WIDTH = N_Q_HEADS * HEAD_DIM
KV_WIDTH = N_KV_HEADS * HEAD_DIM
ROPE_HALF = HEAD_DIM // 2
ROPE_QUARTER = ROPE_HALF // 2
ROPE_THETA = 10000.0
ATTN_SCALE = 1.0 / math.sqrt(HEAD_DIM)
PEER_HEADS = 8
PEER_NKEYS = 128
PEER_HALF = 128
PEER_TOPK = 16
PEER_PICKS = PEER_HEADS * PEER_TOPK

VMEM_LIMIT = 48 * 1024 * 1024


def _tile(n, want):
    t = min(n, want)
    assert n % t == 0, (n, t)
    return t


def _head_mean_square(a, mavg):
    sq = a * a
    hi = sq.astype(BF16)
    lo = (sq - hi.astype(F32)).astype(BF16)
    return (jnp.dot(hi, mavg, preferred_element_type=F32)
            + jnp.dot(lo, mavg, preferred_element_type=F32))


def _in_proj_kernel(x_ref, g1_ref, w_ref, pw_ref, gq_ref, gqs_ref, gk_ref, gks_ref,
                    cos_ref, sin_ref, mavg_ref, yp_ref, q_ref, kt_ref, v_ref):
    x = x_ref[...]
    ms = jnp.mean(x * x, axis=-1, keepdims=True)
    h = (x * lax.rsqrt(ms + EPS) * g1_ref[...]).astype(BF16)
    z = jnp.dot(h, w_ref[...], preferred_element_type=F32)
    for g in range(len(POOL_WINDOWS)):
        sl = slice(g * POOL_GROUP_DIM, (g + 1) * POOL_GROUP_DIM)
        yp_ref[:, sl] = jnp.dot(z[:, sl].astype(BF16), pw_ref[g], preferred_element_type=F32)
    o_q = POOL_WIDTH
    o_k = o_q + ATTN_WIDTH
    o_v = o_k + KV_WIDTH
    o_qs = o_v + KV_WIDTH
    o_ks = o_qs + ATTN_WIDTH
    zq, zk, zv = z[:, o_q:o_k], z[:, o_k:o_v], z[:, o_v:o_qs]
    zqs, zks = z[:, o_qs:o_ks], z[:, o_ks:o_ks + KV_WIDTH]
    cos = cos_ref[...]
    sin = sin_ref[...]
    rq = lax.rsqrt(_head_mean_square(zq, mavg_ref[...]) + EPS)
    q = ((zq * rq * gq_ref[...]) * cos + (zqs * rq * gqs_ref[...]) * sin) * ATTN_SCALE
    q_ref[...] = q.astype(BF16)
    rk = lax.rsqrt(_head_mean_square(zk, mavg_ref[:KV_WIDTH, :KV_WIDTH]) + EPS)
    k = (zk * rk * gk_ref[...]) * cos[:, :KV_WIDTH] + (zks * rk * gks_ref[...]) * sin[:, :KV_WIDTH]
    kt_ref[0] = k.T.astype(BF16)
    v_ref[...] = zv.astype(BF16)


def _in_proj(x2d, n, g1, w_ext, pw, gq, gqs, gk, gks, cos_t, sin_t, mavg):
    N, D = x2d.shape
    T = _tile(n, 512)
    tiles_per_seq = n // T
    const = lambda *s: pl.BlockSpec(s, lambda i: (0,) * len(s))
    return pl.pallas_call(
        _in_proj_kernel,
        grid=(N // T,),
        in_specs=[
            pl.BlockSpec((T, D), lambda i: (i, 0)),
            const(1, D), const(*w_ext.shape), const(*pw.shape),
            const(1, ATTN_WIDTH), const(1, ATTN_WIDTH), const(1, KV_WIDTH), const(1, KV_WIDTH),
            pl.BlockSpec((T, ATTN_WIDTH), lambda i: (i % tiles_per_seq, 0)),
            pl.BlockSpec((T, ATTN_WIDTH), lambda i: (i % tiles_per_seq, 0)),
            const(ATTN_WIDTH, ATTN_WIDTH),
        ],
        out_specs=[
            pl.BlockSpec((T, POOL_WIDTH), lambda i: (i, 0)),
            pl.BlockSpec((T, ATTN_WIDTH), lambda i: (i, 0)),
            pl.BlockSpec((1, KV_WIDTH, T), lambda i: (i // tiles_per_seq, 0, i % tiles_per_seq)),
            pl.BlockSpec((T, KV_WIDTH), lambda i: (i, 0)),
        ],
        out_shape=[
            jax.ShapeDtypeStruct((N, POOL_WIDTH), F32),
            jax.ShapeDtypeStruct((N, ATTN_WIDTH), BF16),
            jax.ShapeDtypeStruct((N // n, KV_WIDTH, n), BF16),
            jax.ShapeDtypeStruct((N, KV_WIDTH), BF16),
        ],
        compiler_params=pltpu.CompilerParams(
            dimension_semantics=("arbitrary",), vmem_limit_bytes=VMEM_LIMIT),
        name="in_proj",
    )(x2d, g1, w_ext, pw, gq, gqs, gk, gks, cos_t, sin_t, mavg)


def _attn_kernel(q_ref, kt_ref, v_ref, o_ref):
    outs = []
    for j in range(N_KV_HEADS):
        kt = kt_ref[0, j * HEAD_DIM:(j + 1) * HEAD_DIM, :]
        v = v_ref[0, :, j * HEAD_DIM:(j + 1) * HEAD_DIM]
        for g in range(GQA_GROUP):
            hq = j * GQA_GROUP + g
            q = q_ref[0, :, hq * HEAD_DIM:(hq + 1) * HEAD_DIM]
            s = jnp.dot(q, kt, preferred_element_type=F32)
            m = jnp.max(s, axis=-1, keepdims=True)
            p = jnp.exp(s - m)
            l = jnp.sum(p, axis=-1, keepdims=True)
            o = jnp.dot(p.astype(BF16), v, preferred_element_type=F32)
            outs.append(o / l)
    o_ref[0] = jnp.concatenate(outs, axis=-1).astype(BF16)


def _attention(q, kt, v):
    B, n, _ = q.shape
    tq = _tile(n, 256)
    return pl.pallas_call(
        _attn_kernel,
        grid=(B, n // tq),
        in_specs=[
            pl.BlockSpec((1, tq, ATTN_WIDTH), lambda b, i: (b, i, 0)),
            pl.BlockSpec((1, KV_WIDTH, n), lambda b, i: (b, 0, 0)),
            pl.BlockSpec((1, n, KV_WIDTH), lambda b, i: (b, 0, 0)),
        ],
        out_specs=pl.BlockSpec((1, tq, ATTN_WIDTH), lambda b, i: (b, i, 0)),
        out_shape=jax.ShapeDtypeStruct((B, n, ATTN_WIDTH), BF16),
        compiler_params=pltpu.CompilerParams(
            dimension_semantics=("arbitrary", "arbitrary"), vmem_limit_bytes=VMEM_LIMIT),
        name="attention",
    )(q, kt, v)


def _out_proj_kernel(n, x_ref, yp_ref, prev_ref, next_ref, at_ref, sc_ref, w_ref, g2_ref,
                     x1_ref, xn_ref):
    T = x_ref.shape[0]
    tiles_per_seq = n // T
    si = pl.program_id(0) % tiles_per_seq
    not_first = (si > 0).astype(F32)
    not_last = (si < tiles_per_seq - 1).astype(F32)
    cur = yp_ref[...]
    ext = jnp.concatenate([prev_ref[...] * not_first, cur, next_ref[...] * not_last], axis=0)
    t = si * T + lax.broadcasted_iota(jnp.int32, (T, 1), 0)
    pooled = []
    for g, w in enumerate(POOL_WINDOWS):
        sl = slice(g * POOL_GROUP_DIM, (g + 1) * POOL_GROUP_DIM)
        eg = ext[:, sl]
        acc = eg[POOL_HALO - w // 2:POOL_HALO - w // 2 + T]
        for d in range(-w // 2 + 1, w - w // 2):
            acc = acc + eg[POOL_HALO + d:POOL_HALO + d + T]
        cnt = jnp.minimum(t + (w - w // 2), n) - jnp.maximum(t - w // 2, 0)
        pooled.append(acc / cnt.astype(F32) - cur[:, sl])
    pool = jnp.concatenate(pooled, axis=-1) * sc_ref[...]
    mix = jnp.concatenate([pool.astype(BF16), at_ref[...]], axis=-1)
    x1 = x_ref[...] + jnp.dot(mix, w_ref[...], preferred_element_type=F32)
    x1_ref[...] = x1
    ms = jnp.mean(x1 * x1, axis=-1, keepdims=True)
    xn_ref[...] = (x1 * lax.rsqrt(ms + EPS) * g2_ref[...]).astype(BF16)


def _out_proj(x2d, n, yp, attn, scale, w_out, g2):
    N, D = x2d.shape
    T = _tile(n, 512)
    hb = T // POOL_HALO
    last_hb = N // POOL_HALO - 1
    const = lambda *s: pl.BlockSpec(s, lambda i: (0,) * len(s))
    return pl.pallas_call(
        functools.partial(_out_proj_kernel, n),
        grid=(N // T,),
        in_specs=[
            pl.BlockSpec((T, D), lambda i: (i, 0)),
            pl.BlockSpec((T, POOL_WIDTH), lambda i: (i, 0)),
            pl.BlockSpec((POOL_HALO, POOL_WIDTH), lambda i: (jnp.maximum(i * hb - 1, 0), 0)),
            pl.BlockSpec((POOL_HALO, POOL_WIDTH), lambda i: (jnp.minimum((i + 1) * hb, last_hb), 0)),
            pl.BlockSpec((T, ATTN_WIDTH), lambda i: (i, 0)),
            const(1, POOL_WIDTH), const(*w_out.shape), const(1, D),
        ],
        out_specs=[
            pl.BlockSpec((T, D), lambda i: (i, 0)),
            pl.BlockSpec((T, D), lambda i: (i, 0)),
        ],
        out_shape=[
            jax.ShapeDtypeStruct((N, D), F32),
            jax.ShapeDtypeStruct((N, D), BF16),
        ],
        compiler_params=pltpu.CompilerParams(
            dimension_semantics=("arbitrary",), vmem_limit_bytes=VMEM_LIMIT),
        name="out_proj",
    )(x2d, yp, yp, yp, attn, scale, w_out, g2)


def _extract_top(vals, payload, k):
    R = vals.shape[0]
    rows = lax.broadcasted_iota(jnp.int32, vals.shape, 0).astype(F32)
    top_v, top_p = [], []
    for _ in range(k):
        m = jnp.max(vals, axis=0, keepdims=True)
        pos = jnp.min(jnp.where(vals == m, rows, float(R)), axis=0, keepdims=True)
        sel = rows == pos
        top_v.append(m)
        if payload is None:
            top_p.append(pos)
        else:
            top_p.append(jnp.max(jnp.where(sel, payload, -1.0), axis=0, keepdims=True))
        vals = jnp.where(sel, -jnp.inf, vals)
    return jnp.concatenate(top_v, axis=0), jnp.concatenate(top_p, axis=0)


def _pair_candidates(v0, i0, v1, i1):
    K = PEER_TOPK
    sub = lax.broadcasted_iota(jnp.int32, (8, v0.shape[1]), 0)
    vals = [v0[0:1] + v1, v0[1:2] + v1[0:8]]
    eids = [i0[0:1] * PEER_NKEYS + i1, i0[1:2] * PEER_NKEYS + i1[0:8]]
    for a in range(2, 8):
        keep = sub < K // (a + 1)
        vals.append(jnp.where(keep, v0[a:a + 1] + v1[0:8], -jnp.inf))
        eids.append(i0[a:a + 1] * PEER_NKEYS + i1[0:8])
    vals.append(v0[8:16] + v1[0:1])
    eids.append(i0[8:16] * PEER_NKEYS + i1[0:1])
    return jnp.concatenate(vals, axis=0), jnp.concatenate(eids, axis=0)


def _peer_route_kernel(xn_ref, wqt_ref, sk_ref, e_ref, gt_ref, qt_scr, et_scr):
    T = xn_ref.shape[0]
    qt_scr[...] = lax.dot_general(wqt_ref[...], xn_ref[...], (((1,), (1,)), ((), ())),
                                  preferred_element_type=F32).astype(BF16)
    def route_head(h):
        sub_v, sub_i = [], []
        for p in range(2):
            hp = h * 2 + p
            qhp = qt_scr[pl.ds(pl.multiple_of(hp * PEER_HALF, PEER_HALF), PEER_HALF), :]
            s = jnp.dot(sk_ref[hp], qhp, preferred_element_type=F32)
            tv, ti = _extract_top(s, None, PEER_TOPK)
            sub_v.append(tv)
            sub_i.append(ti)
        comb, eid = _pair_candidates(sub_v[0], sub_i[0], sub_v[1], sub_i[1])
        cv, ce = _extract_top(comb, eid, PEER_TOPK)
        ex = jnp.exp(cv - cv[0:1, :])
        gate = ex / jnp.sum(ex, axis=0, keepdims=True)
        row0 = pl.multiple_of(h * PEER_TOPK, PEER_TOPK)
        et_scr[pl.ds(row0, PEER_TOPK), :] = ce.astype(F32)
        gt_ref[0, pl.ds(row0, PEER_TOPK), :] = gate

    def heads_body(c, carry):
        for r in range(ROUTE_HEADS_PER_ITER):
            route_head(c * ROUTE_HEADS_PER_ITER + r)
        return carry

    lax.fori_loop(0, PEER_HEADS // ROUTE_HEADS_PER_ITER, heads_body, 0)
    e_ref[...] = et_scr[...].T.astype(jnp.int32)


PEER_ROUTE_TILE = 128
ROUTE_HEADS_PER_ITER = 8


def _peer_route(xn, wqt, sk):
    N, D = xn.shape
    T = PEER_ROUTE_TILE
    const = lambda *s: pl.BlockSpec(s, lambda i: (0,) * len(s))
    return pl.pallas_call(
        _peer_route_kernel,
        grid=(N // T,),
        in_specs=[pl.BlockSpec((T, D), lambda i: (i, 0)), const(*wqt.shape), const(*sk.shape)],
        out_specs=[
            pl.BlockSpec((T, PEER_PICKS), lambda i: (i, 0)),
            pl.BlockSpec((1, PEER_PICKS, T), lambda i: (i, 0, 0)),
        ],
        out_shape=[
            jax.ShapeDtypeStruct((N, PEER_PICKS), jnp.int32),
            jax.ShapeDtypeStruct((N // T, PEER_PICKS, T), F32),
        ],
        scratch_shapes=[
            pltpu.VMEM((wqt.shape[0], T), BF16),
            pltpu.VMEM((PEER_PICKS, T), F32),
        ],
        compiler_params=pltpu.CompilerParams(
            dimension_semantics=("arbitrary",), vmem_limit_bytes=VMEM_LIMIT),
        name="peer_route",
    )(xn, wqt, sk)


PEER_TOK_TILE = 16
PEER_PROLOGUE_UNROLL = 16
ROW_SUBLANES = 8
ISSUE_BEFORE_WAIT = 512
ISSUE_PER_GROUP = 3
ISSUE_PER_GATHER_STEP = 8
U_HALF_MASK = -65536


def _sum_sublanes_of_8(ps, sub):
    def comb(a, b, h):
        m = (sub & h) == 0
        if 2 * h == ROW_SUBLANES:
            return jnp.where(m, a, b) + pltpu.roll(jnp.where(m, b, a), h, 0)
        return (jnp.where(m, a, pltpu.roll(b, h, 0))
                + jnp.where(m, pltpu.roll(a, ROW_SUBLANES - h, 0), b))
    l1 = [comb(ps[2 * j], ps[2 * j + 1], 1) for j in range(4)]
    l2 = [comb(l1[0], l1[1], 2), comb(l1[2], l1[3], 2)]
    return comb(l2[0], l2[1], 4)


def _peer_ffn_kernel(ids_cur_ref, ids_nxt_ref, x1_ref, g2_ref, gt_ref, tab_ref, y_ref,
                     buf_a, buf_b, hs_scr, ab_scr, sem):
    TT = PEER_TOK_TILE
    rows = TT * PEER_PICKS
    D = ROW_SUBLANES * 128
    i = pl.program_id(0)
    sub = lax.broadcasted_iota(jnp.int32, (ROW_SUBLANES, 128), 0)
    lane_ids = lax.broadcasted_iota(jnp.int32, (PEER_PICKS, PEER_ROUTE_TILE), 1)
    g2 = g2_ref[...]

    def start_row(ids_ref, j_ids, dst, dst_sem, j, prio):
        pltpu.make_async_copy(tab_ref.at[ids_ref[0, 0, j_ids]], dst.at[j], dst_sem).start(priority=prio)

    def wait_tile(dst, dst_sem):
        pltpu.make_async_copy(dst, dst, dst_sem).wait()

    def process(cur, cur_sem, nxt, nxt_sem, nxt_ids_ref, nxt_ids_off, tok0):
        pending = iter(range(rows))

        def issue(n):
            for _ in range(n):
                k = next(pending, None)
                if k is not None:
                    start_row(nxt_ids_ref, nxt_ids_off + k, nxt, nxt_sem, k, k % 2)

        issue(ISSUE_BEFORE_WAIT)
        wait_tile(cur, cur_sem)
        lanes = lane_ids - ((i * 2 * TT) % PEER_ROUTE_TILE + tok0)

        for t in range(TT):
            base = t * PEER_PICKS
            x1 = x1_ref[tok0 + t]
            ms = jnp.sum(x1 * x1, axis=(0, 1), keepdims=True) * (1.0 / D)
            xt = x1 * lax.rsqrt(ms + EPS) * g2
            groups = []
            for g in range(PEER_PICKS // ROW_SUBLANES):
                issue(ISSUE_PER_GROUP)
                ps = []
                for j in range(ROW_SUBLANES):
                    w = cur[base + g * ROW_SUBLANES + j]
                    ps.append(lax.bitcast_convert_type(w & jnp.int32(U_HALF_MASK), F32) * xt)
                groups.append(_sum_sublanes_of_8(ps, sub))
            hs_scr[t] = jnp.concatenate(groups, axis=0)

        hid = jnp.zeros((PEER_PICKS, PEER_ROUTE_TILE), F32)
        for t in range(TT):
            issue(ISSUE_PER_GATHER_STEP)
            hid = jnp.where(lanes == t, jnp.sum(hs_scr[t], axis=-1, keepdims=True), hid)
        act = 0.5 * hid * (1.0 + lax.erf(hid * (1.0 / math.sqrt(2.0)))) * gt_ref[0]
        for t in range(TT):
            issue(ISSUE_PER_GATHER_STEP)
            a_t = jnp.sum(jnp.where(lanes == t, act, 0.0), axis=-1, keepdims=True)
            ab_scr[t] = jnp.broadcast_to(a_t, ab_scr.shape[1:])

        for t in range(TT):
            base = t * PEER_PICKS
            accs = [jnp.zeros((ROW_SUBLANES, 128), F32) for _ in range(4)]
            for k in range(PEER_PICKS):
                if k % ROW_SUBLANES < ISSUE_PER_GROUP:
                    issue(1)
                gv = lax.bitcast_convert_type(cur[base + k] << 16, F32)
                accs[k % 4] = accs[k % 4] + gv * jnp.broadcast_to(ab_scr[t, k:k + 1, :], gv.shape)
            y_ref[tok0 + t] = x1_ref[tok0 + t] + ((accs[0] + accs[1]) + (accs[2] + accs[3]))
        issue(rows)

    @pl.when(i == 0)
    def _():
        def body(c, carry):
            for r in range(PEER_PROLOGUE_UNROLL):
                j = c * PEER_PROLOGUE_UNROLL + r
                start_row(ids_cur_ref, j, buf_a, sem.at[0], j, r % 2)
            return carry
        lax.fori_loop(0, rows // PEER_PROLOGUE_UNROLL, body, 0)

    process(buf_a, sem.at[0], buf_b, sem.at[1], ids_cur_ref, rows, 0)
    process(buf_b, sem.at[1], buf_a, sem.at[0], ids_nxt_ref, 0, TT)

    @pl.when(i == pl.num_programs(0) - 1)
    def _():
        wait_tile(buf_a, sem.at[0])


def _peer_ffn(ids, gt, x1_rows, g2_rows, table):
    N = x1_rows.shape[0]
    TT = PEER_TOK_TILE
    ns = N // (2 * TT)
    rows = TT * PEER_PICKS
    ids3 = ids.reshape(ns, 1, 2 * rows)
    tok = pl.BlockSpec((2 * TT, ROW_SUBLANES, 128), lambda i: (i, 0, 0))
    return pl.pallas_call(
        _peer_ffn_kernel,
        grid=(ns,),
        in_specs=[
            pl.BlockSpec((1, 1, 2 * rows), lambda i: (i, 0, 0), memory_space=pltpu.SMEM),
            pl.BlockSpec((1, 1, 2 * rows), lambda i: (jnp.minimum(i + 1, ns - 1), 0, 0),
                         memory_space=pltpu.SMEM),
            tok,
            pl.BlockSpec((ROW_SUBLANES, 128), lambda i: (0, 0)),
            pl.BlockSpec((1, PEER_PICKS, PEER_ROUTE_TILE),
                         lambda i: (i * 2 * TT // PEER_ROUTE_TILE, 0, 0)),
            pl.BlockSpec(memory_space=pl.ANY),
        ],
        out_specs=tok,
        out_shape=jax.ShapeDtypeStruct(x1_rows.shape, F32),
        scratch_shapes=[
            pltpu.VMEM((rows, ROW_SUBLANES, 128), jnp.int32),
            pltpu.VMEM((rows, ROW_SUBLANES, 128), jnp.int32),
            pltpu.VMEM((TT, PEER_PICKS, 128), F32),
            pltpu.VMEM((TT, PEER_PICKS, 128), F32),
            pltpu.SemaphoreType.DMA((2,)),
        ],
        compiler_params=pltpu.CompilerParams(
            dimension_semantics=("arbitrary",), vmem_limit_bytes=VMEM_LIMIT),
        name="peer_ffn",
    )(ids3, ids3, x1_rows, g2_rows, gt, table)


def _rope_partner(width):
    c = np.arange(width)
    j = c % ROPE_HALF
    return np.where(j < ROPE_QUARTER, c + ROPE_QUARTER, c - ROPE_QUARTER)


def _rope_tables(n):
    rows = n // GRID_W
    row = jnp.repeat(jnp.arange(rows, dtype=F32), GRID_W)
    col = jnp.tile(jnp.arange(GRID_W, dtype=F32), rows)
    inv = 1.0 / (ROPE_THETA ** (jnp.arange(0, ROPE_HALF, 2, dtype=F32) / ROPE_HALF))
    ang_r = row[:, None] * inv
    ang_c = col[:, None] * inv
    cos = jnp.concatenate([jnp.cos(ang_r)] * 2 + [jnp.cos(ang_c)] * 2, axis=-1)
    sin = jnp.concatenate([-jnp.sin(ang_r), jnp.sin(ang_r), -jnp.sin(ang_c), jnp.sin(ang_c)], axis=-1)
    return jnp.tile(cos, (1, N_Q_HEADS)), jnp.tile(sin, (1, N_Q_HEADS))


def _layer(x, p):
    B, n, D = x.shape
    N = B * n
    x2d = x.reshape(N, D)
    cos_t, sin_t = _rope_tables(n)
    yp, q, kt, v = _in_proj(x2d, n, p["g1"], p["w_ext"], p["pw"], p["gq"], p["gqs"], p["gk"],
                            p["gks"], cos_t, sin_t, p["mavg"])
    attn = _attention(q.reshape(B, n, ATTN_WIDTH), kt, v.reshape(B, n, KV_WIDTH))
    x1, xn = _out_proj(x2d, n, yp, attn.reshape(N, ATTN_WIDTH), p["scale"], p["w_out"], p["g2"])
    ids, gt = _peer_route(xn, p["wqt"], p["sk"])
    y = _peer_ffn(ids, gt, x1.reshape(N, ROW_SUBLANES, 128), p["g2"].reshape(ROW_SUBLANES, 128),
                  p["table"])
    return y.reshape(B, n, D)


def _prepare(norm1_g, w_in, pool_w, pool_scale, q_norm_g, k_norm_g, w_out, norm2_g,
             peer_wq, peer_subkeys, peer_u, peer_v):
    pq = _rope_partner(ATTN_WIDTH)
    pk = _rope_partner(KV_WIDTH)
    o_q = POOL_WIDTH
    o_k = o_q + ATTN_WIDTH
    w_ext = jnp.concatenate([w_in, w_in[:, o_q + pq], w_in[:, o_k + pk]], axis=1).astype(BF16)
    gq = jnp.tile(q_norm_g, N_Q_HEADS)
    gk = jnp.tile(k_norm_g, N_KV_HEADS)
    blk = np.arange(ATTN_WIDTH) // HEAD_DIM
    mavg = jnp.asarray((blk[:, None] == blk[None, :]) / HEAD_DIM, BF16)
    ub = lax.bitcast_convert_type(peer_u.astype(BF16), jnp.uint16).astype(jnp.uint32)
    vb = lax.bitcast_convert_type(peer_v.astype(BF16), jnp.uint16).astype(jnp.uint32)
    table = lax.bitcast_convert_type((ub << 16) | vb, jnp.int32).reshape(-1, ROW_SUBLANES, 128)
    return dict(
        g1=norm1_g[None, :], w_ext=w_ext, pw=pool_w.astype(BF16),
        gq=gq[None, :], gqs=gq[pq][None, :], gk=gk[None, :], gks=gk[pk][None, :], mavg=mavg,
        scale=pool_scale[None, :], w_out=w_out.astype(BF16), g2=norm2_g[None, :],
        wqt=peer_wq.T.astype(BF16),
        sk=peer_subkeys.reshape(PEER_HEADS * 2, PEER_NKEYS, PEER_HALF).astype(BF16),
        table=table,
    )


def kernel(x_prompt, x_sample, norm1_g, w_in, pool_w, pool_scale, q_norm_g, k_norm_g, w_out,
           norm2_g, peer_wq, peer_subkeys, peer_u, peer_v):
    y_prompt, y_sample = x_prompt, x_sample
    for l in range(norm1_g.shape[0]):
        p = _prepare(norm1_g[l], w_in[l], pool_w[l], pool_scale[l], q_norm_g[l], k_norm_g[l],
                     w_out[l], norm2_g[l], peer_wq[l], peer_subkeys[l], peer_u[l], peer_v[l])
        y_prompt = _layer(y_prompt, p)
        y_sample = _layer(y_sample, p)
    return (y_prompt, y_sample)
```

```python
import functools
import math

import numpy as np
import jax
import jax.numpy as jnp
from jax import lax
from jax.experimental import pallas as pl
from jax.experimental.pallas import tpu as pltpu

F32 = jnp.float32
BF16 = jnp.bfloat16

EPS = 1e-6
GRID_W = 64
POOL_WINDOWS = (2, 4, 8, 16)
POOL_GROUP_DIM = 128
POOL_WIDTH = 512
POOL_HALO = 8
HEAD_DIM = 64
N_Q_HEADS = 8
N_KV_HEADS = 2
GQA_GROUP = N_Q_HEADS // N_KV_HEADS
ATTN_WIDTH = N_Q_HEADS * HEAD_DIM
KV_WIDTH = N_KV_HEADS * HEAD_DIM
ROPE_HALF = HEAD_DIM // 2
ROPE_QUARTER = ROPE_HALF // 2
ROPE_THETA = 10000.0
ATTN_SCALE = 1.0 / math.sqrt(HEAD_DIM)
LOG2_E = math.log2(math.e)
PEER_HEADS = 8
PEER_NKEYS = 128
PEER_HALF = 128
PEER_TOPK = 16
PEER_PICKS = PEER_HEADS * PEER_TOPK

VMEM_LIMIT = 48 * 1024 * 1024


def _tile(n, want):
    t = min(n, want)
    assert n % t == 0, (n, t)
    return t


def _head_mean_square(a, mavg):
    sq = a * a
    hi = sq.astype(BF16)
    lo = (sq - hi.astype(F32)).astype(BF16)
    return (jnp.dot(hi, mavg, preferred_element_type=F32)
            + jnp.dot(lo, mavg, preferred_element_type=F32))


def _in_proj_kernel(x_ref, g1_ref, w_ref, pw_ref, gq_ref, gqs_ref, gk_ref, gks_ref,
                    cos_ref, sin_ref, mavg_ref, yp_ref, qt_ref, k_ref, vt_ref):
    x = x_ref[...]
    ms = jnp.mean(x * x, axis=-1, keepdims=True)
    h = (x * lax.rsqrt(ms + EPS) * g1_ref[...]).astype(BF16)
    z = jnp.dot(h, w_ref[...], preferred_element_type=F32)
    for g in range(len(POOL_WINDOWS)):
        sl = slice(g * POOL_GROUP_DIM, (g + 1) * POOL_GROUP_DIM)
        yp_ref[:, sl] = jnp.dot(z[:, sl].astype(BF16), pw_ref[g], preferred_element_type=F32)
    o_q = POOL_WIDTH
    o_k = o_q + ATTN_WIDTH
    o_v = o_k + KV_WIDTH
    o_qs = o_v + KV_WIDTH
    o_ks = o_qs + ATTN_WIDTH
    zq, zk, zv = z[:, o_q:o_k], z[:, o_k:o_v], z[:, o_v:o_qs]
    zqs, zks = z[:, o_qs:o_ks], z[:, o_ks:o_ks + KV_WIDTH]
    cos = cos_ref[...]
    sin = sin_ref[...]
    rq = lax.rsqrt(_head_mean_square(zq, mavg_ref[...]) + EPS)
    q = ((zq * rq * gq_ref[...]) * cos + (zqs * rq * gqs_ref[...]) * sin) * (ATTN_SCALE * LOG2_E)
    qt_ref[0] = q.T.astype(BF16)
    rk = lax.rsqrt(_head_mean_square(zk, mavg_ref[:KV_WIDTH, :KV_WIDTH]) + EPS)
    k = (zk * rk * gk_ref[...]) * cos[:, :KV_WIDTH] + (zks * rk * gks_ref[...]) * sin[:, :KV_WIDTH]
    k_ref[...] = k.astype(BF16)
    vt_ref[0] = zv.T.astype(BF16)


def _in_proj(x2d, n, g1, w_ext, pw, gq, gqs, gk, gks, cos_t, sin_t, mavg):
    N, D = x2d.shape
    T = _tile(n, 512)
    tiles_per_seq = n // T
    const = lambda *s: pl.BlockSpec(s, lambda i: (0,) * len(s))
    return pl.pallas_call(
        _in_proj_kernel,
        grid=(N // T,),
        in_specs=[
            pl.BlockSpec((T, D), lambda i: (i, 0)),
            const(1, D), const(*w_ext.shape), const(*pw.shape),
            const(1, ATTN_WIDTH), const(1, ATTN_WIDTH), const(1, KV_WIDTH), const(1, KV_WIDTH),
            pl.BlockSpec((T, ATTN_WIDTH), lambda i: (i % tiles_per_seq, 0)),
            pl.BlockSpec((T, ATTN_WIDTH), lambda i: (i % tiles_per_seq, 0)),
            const(ATTN_WIDTH, ATTN_WIDTH),
        ],
        out_specs=[
            pl.BlockSpec((T, POOL_WIDTH), lambda i: (i, 0)),
            pl.BlockSpec((1, ATTN_WIDTH, T), lambda i: (i // tiles_per_seq, 0, i % tiles_per_seq)),
            pl.BlockSpec((T, KV_WIDTH), lambda i: (i, 0)),
            pl.BlockSpec((1, KV_WIDTH, T), lambda i: (i // tiles_per_seq, 0, i % tiles_per_seq)),
        ],
        out_shape=[
            jax.ShapeDtypeStruct((N, POOL_WIDTH), F32),
            jax.ShapeDtypeStruct((N // n, ATTN_WIDTH, n), BF16),
            jax.ShapeDtypeStruct((N, KV_WIDTH), BF16),
            jax.ShapeDtypeStruct((N // n, KV_WIDTH, n), BF16),
        ],
        compiler_params=pltpu.CompilerParams(
            dimension_semantics=("arbitrary",), vmem_limit_bytes=VMEM_LIMIT),
        name="in_proj",
    )(x2d, g1, w_ext, pw, gq, gqs, gk, gks, cos_t, sin_t, mavg)


def _attn_kernel(qt_ref, k_ref, vt_ref, ot_ref, st_scr):
    def head_rows(h):
        return slice(h * HEAD_DIM, (h + 1) * HEAD_DIM)

    def scores(h):
        j = h // GQA_GROUP
        k = k_ref[0, :, j * HEAD_DIM:(j + 1) * HEAD_DIM]
        st_scr[h % 2] = jnp.dot(k, qt_ref[0, head_rows(h), :], preferred_element_type=F32)

    scores(0)
    for h in range(N_Q_HEADS):
        if h + 1 < N_Q_HEADS:
            scores(h + 1)
        j = h // GQA_GROUP
        vt = vt_ref[0, j * HEAD_DIM:(j + 1) * HEAD_DIM, :]
        st = st_scr[h % 2]
        m = jnp.max(st, axis=0, keepdims=True)
        pt = jnp.exp2(st - m)
        l = jnp.sum(pt, axis=0, keepdims=True)
        ot = jnp.dot(vt, pt.astype(BF16), preferred_element_type=F32)
        ot_ref[0, head_rows(h), :] = (ot / l).astype(BF16)


def _attention(qt, k, vt):
    B, _, n = qt.shape
    tq = _tile(n, 256)
    return pl.pallas_call(
        _attn_kernel,
        grid=(B, n // tq),
        in_specs=[
            pl.BlockSpec((1, ATTN_WIDTH, tq), lambda b, i: (b, 0, i)),
            pl.BlockSpec((1, n, KV_WIDTH), lambda b, i: (b, 0, 0)),
            pl.BlockSpec((1, KV_WIDTH, n), lambda b, i: (b, 0, 0)),
        ],
        out_specs=pl.BlockSpec((1, ATTN_WIDTH, tq), lambda b, i: (b, 0, i)),
        out_shape=jax.ShapeDtypeStruct((B, ATTN_WIDTH, n), BF16),
        scratch_shapes=[pltpu.VMEM((2, n, tq), F32)],
        compiler_params=pltpu.CompilerParams(
            dimension_semantics=("arbitrary", "arbitrary"), vmem_limit_bytes=VMEM_LIMIT),
        name="attention",
    )(qt, k, vt)


def _out_proj_kernel(n, x_ref, yp_ref, prev_ref, next_ref, at_ref, sc_ref, w_ref, g2_ref,
                     x1_ref, xn_ref):
    T = x_ref.shape[0]
    tiles_per_seq = n // T
    si = pl.program_id(0) % tiles_per_seq
    not_first = (si > 0).astype(F32)
    not_last = (si < tiles_per_seq - 1).astype(F32)
    cur = yp_ref[...]
    ext = jnp.concatenate([prev_ref[...] * not_first, cur, next_ref[...] * not_last], axis=0)
    t = si * T + lax.broadcasted_iota(jnp.int32, (T, 1), 0)
    pooled = []
    for g, w in enumerate(POOL_WINDOWS):
        sl = slice(g * POOL_GROUP_DIM, (g + 1) * POOL_GROUP_DIM)
        eg = ext[:, sl]
        acc = eg[POOL_HALO - w // 2:POOL_HALO - w // 2 + T]
        for d in range(-w // 2 + 1, w - w // 2):
            acc = acc + eg[POOL_HALO + d:POOL_HALO + d + T]
        cnt = jnp.minimum(t + (w - w // 2), n) - jnp.maximum(t - w // 2, 0)
        pooled.append(acc / cnt.astype(F32) - cur[:, sl])
    pool = (jnp.concatenate(pooled, axis=-1) * sc_ref[...]).astype(BF16)
    mixed = (jnp.dot(pool, w_ref[:POOL_WIDTH, :], preferred_element_type=F32)
             + lax.dot_general(at_ref[0], w_ref[POOL_WIDTH:, :], (((0,), (0,)), ((), ())),
                               preferred_element_type=F32))
    x1 = x_ref[...] + mixed
    x1_ref[...] = x1
    ms = jnp.mean(x1 * x1, axis=-1, keepdims=True)
    xn_ref[...] = (x1 * lax.rsqrt(ms + EPS) * g2_ref[...]).astype(BF16)


def _out_proj(x2d, n, yp, attn, scale, w_out, g2):
    N, D = x2d.shape
    T = _tile(n, 512)
    tiles_per_seq = n // T
    hb = T // POOL_HALO
    last_hb = N // POOL_HALO - 1
    const = lambda *s: pl.BlockSpec(s, lambda i: (0,) * len(s))
    return pl.pallas_call(
        functools.partial(_out_proj_kernel, n),
        grid=(N // T,),
        in_specs=[
            pl.BlockSpec((T, D), lambda i: (i, 0)),
            pl.BlockSpec((T, POOL_WIDTH), lambda i: (i, 0)),
            pl.BlockSpec((POOL_HALO, POOL_WIDTH), lambda i: (jnp.maximum(i * hb - 1, 0), 0)),
            pl.BlockSpec((POOL_HALO, POOL_WIDTH), lambda i: (jnp.minimum((i + 1) * hb, last_hb), 0)),
            pl.BlockSpec((1, ATTN_WIDTH, T), lambda i: (i // tiles_per_seq, 0, i % tiles_per_seq)),
            const(1, POOL_WIDTH), const(*w_out.shape), const(1, D),
        ],
        out_specs=[
            pl.BlockSpec((T, D), lambda i: (i, 0)),
            pl.BlockSpec((T, D), lambda i: (i, 0)),
        ],
        out_shape=[
            jax.ShapeDtypeStruct((N, D), F32),
            jax.ShapeDtypeStruct((N, D), BF16),
        ],
        compiler_params=pltpu.CompilerParams(
            dimension_semantics=("arbitrary",), vmem_limit_bytes=VMEM_LIMIT),
        name="out_proj",
    )(x2d, yp, yp, yp, attn, scale, w_out, g2)


def _extract_top(vals, payload, k):
    R = vals.shape[0]
    rows = lax.broadcasted_iota(jnp.int32, vals.shape, 0).astype(F32)
    top_v, top_p = [], []
    for _ in range(k):
        m = jnp.max(vals, axis=0, keepdims=True)
        pos = jnp.min(jnp.where(vals == m, rows, float(R)), axis=0, keepdims=True)
        sel = rows == pos
        top_v.append(m)
        if payload is None:
            top_p.append(pos)
        else:
            top_p.append(jnp.max(jnp.where(sel, payload, -1.0), axis=0, keepdims=True))
        vals = jnp.where(sel, -jnp.inf, vals)
    return jnp.concatenate(top_v, axis=0), jnp.concatenate(top_p, axis=0)


def _pair_candidates(v0, i0, v1, i1):
    K = PEER_TOPK
    sub = lax.broadcasted_iota(jnp.int32, (8, v0.shape[1]), 0)
    vals = [v0[0:1] + v1, v0[1:2] + v1[0:8]]
    eids = [i0[0:1] * PEER_NKEYS + i1, i0[1:2] * PEER_NKEYS + i1[0:8]]
    for a in range(2, 8):
        keep = sub < K // (a + 1)
        vals.append(jnp.where(keep, v0[a:a + 1] + v1[0:8], -jnp.inf))
        eids.append(i0[a:a + 1] * PEER_NKEYS + i1[0:8])
    vals.append(v0[8:16] + v1[0:1])
    eids.append(i0[8:16] * PEER_NKEYS + i1[0:1])
    return jnp.concatenate(vals, axis=0), jnp.concatenate(eids, axis=0)


def _peer_route_kernel(xn_ref, wqt_ref, sk_ref, e_ref, gt_ref, qt_scr, et_scr):
    T = xn_ref.shape[0]
    qt_scr[...] = lax.dot_general(wqt_ref[...], xn_ref[...], (((1,), (1,)), ((), ())),
                                  preferred_element_type=F32).astype(BF16)
    def route_head(h):
        sub_v, sub_i = [], []
        for p in range(2):
            hp = h * 2 + p
            qhp = qt_scr[pl.ds(pl.multiple_of(hp * PEER_HALF, PEER_HALF), PEER_HALF), :]
            s = jnp.dot(sk_ref[hp], qhp, preferred_element_type=F32)
            tv, ti = _extract_top(s, None, PEER_TOPK)
            sub_v.append(tv)
            sub_i.append(ti)
        comb, eid = _pair_candidates(sub_v[0], sub_i[0], sub_v[1], sub_i[1])
        cv, ce = _extract_top(comb, eid, PEER_TOPK)
        ex = jnp.exp(cv - cv[0:1, :])
        gate = ex / jnp.sum(ex, axis=0, keepdims=True)
        row0 = pl.multiple_of(h * PEER_TOPK, PEER_TOPK)
        et_scr[pl.ds(row0, PEER_TOPK), :] = ce.astype(F32)
        gt_ref[0, pl.ds(row0, PEER_TOPK), :] = gate

    def heads_body(c, carry):
        for r in range(ROUTE_HEADS_PER_ITER):
            route_head(c * ROUTE_HEADS_PER_ITER + r)
        return carry

    lax.fori_loop(0, PEER_HEADS // ROUTE_HEADS_PER_ITER, heads_body, 0)
    e_ref[...] = et_scr[...].T.astype(jnp.int32)


PEER_ROUTE_TILE = 128
ROUTE_HEADS_PER_ITER = 8


def _peer_route(xn, wqt, sk):
    N, D = xn.shape
    T = PEER_ROUTE_TILE
    const = lambda *s: pl.BlockSpec(s, lambda i: (0,) * len(s))
    return pl.pallas_call(
        _peer_route_kernel,
        grid=(N // T,),
        in_specs=[pl.BlockSpec((T, D), lambda i: (i, 0)), const(*wqt.shape), const(*sk.shape)],
        out_specs=[
            pl.BlockSpec((T, PEER_PICKS), lambda i: (i, 0)),
            pl.BlockSpec((1, PEER_PICKS, T), lambda i: (i, 0, 0)),
        ],
        out_shape=[
            jax.ShapeDtypeStruct((N, PEER_PICKS), jnp.int32),
            jax.ShapeDtypeStruct((N // T, PEER_PICKS, T), F32),
        ],
        scratch_shapes=[
            pltpu.VMEM((wqt.shape[0], T), BF16),
            pltpu.VMEM((PEER_PICKS, T), F32),
        ],
        compiler_params=pltpu.CompilerParams(
            dimension_semantics=("arbitrary",), vmem_limit_bytes=VMEM_LIMIT),
        name="peer_route",
    )(xn, wqt, sk)


PEER_TOK_TILE = 16
PEER_PROLOGUE_UNROLL = 16
ROW_SUBLANES = 8
ISSUE_BEFORE_WAIT = 512
ISSUE_PER_GROUP = 3
ISSUE_PER_GATHER_STEP = 8
U_HALF_MASK = -65536


def _sum_sublanes_of_8(ps, sub):
    def comb(a, b, h):
        m = (sub & h) == 0
        if 2 * h == ROW_SUBLANES:
            return jnp.where(m, a, b) + pltpu.roll(jnp.where(m, b, a), h, 0)
        return (jnp.where(m, a, pltpu.roll(b, h, 0))
                + jnp.where(m, pltpu.roll(a, ROW_SUBLANES - h, 0), b))
    l1 = [comb(ps[2 * j], ps[2 * j + 1], 1) for j in range(4)]
    l2 = [comb(l1[0], l1[1], 2), comb(l1[2], l1[3], 2)]
    return comb(l2[0], l2[1], 4)


def _peer_ffn_kernel(ids_cur_ref, ids_nxt_ref, x1_ref, g2_ref, gt_ref, tab_ref, y_ref,
                     buf_a, buf_b, hs_scr, ab_scr, sem):
    TT = PEER_TOK_TILE
    rows = TT * PEER_PICKS
    D = ROW_SUBLANES * 128
    i = pl.program_id(0)
    sub = lax.broadcasted_iota(jnp.int32, (ROW_SUBLANES, 128), 0)
    lane_ids = lax.broadcasted_iota(jnp.int32, (PEER_PICKS, PEER_ROUTE_TILE), 1)
    g2 = g2_ref[...]

    def start_row(ids_ref, j_ids, dst, dst_sem, j, prio):
        pltpu.make_async_copy(tab_ref.at[ids_ref[0, 0, j_ids]], dst.at[j], dst_sem).start(priority=prio)

    def wait_tile(dst, dst_sem):
        pltpu.make_async_copy(dst, dst, dst_sem).wait()

    def process(cur, cur_sem, nxt, nxt_sem, nxt_ids_ref, nxt_ids_off, tok0):
        pending = iter(range(rows))

        def issue(n):
            for _ in range(n):
                k = next(pending, None)
                if k is not None:
                    start_row(nxt_ids_ref, nxt_ids_off + k, nxt, nxt_sem, k, k % 2)

        issue(ISSUE_BEFORE_WAIT)
        wait_tile(cur, cur_sem)
        lanes = lane_ids - ((i * 2 * TT) % PEER_ROUTE_TILE + tok0)

        for t in range(TT):
            base = t * PEER_PICKS
            x1 = x1_ref[tok0 + t]
            ms = jnp.sum(x1 * x1, axis=(0, 1), keepdims=True) * (1.0 / D)
            xt = x1 * lax.rsqrt(ms + EPS) * g2
            groups = []
            for g in range(PEER_PICKS // ROW_SUBLANES):
                issue(ISSUE_PER_GROUP)
                ps = []
                for j in range(ROW_SUBLANES):
                    w = cur[base + g * ROW_SUBLANES + j]
                    ps.append(lax.bitcast_convert_type(w & jnp.int32(U_HALF_MASK), F32) * xt)
                groups.append(_sum_sublanes_of_8(ps, sub))
            hs_scr[t] = jnp.concatenate(groups, axis=0)

        hid = jnp.zeros((PEER_PICKS, PEER_ROUTE_TILE), F32)
        for t in range(TT):
            issue(ISSUE_PER_GATHER_STEP)
            hid = jnp.where(lanes == t, jnp.sum(hs_scr[t], axis=-1, keepdims=True), hid)
        act = 0.5 * hid * (1.0 + lax.erf(hid * (1.0 / math.sqrt(2.0)))) * gt_ref[0]
        for t in range(TT):
            issue(ISSUE_PER_GATHER_STEP)
            a_t = jnp.sum(jnp.where(lanes == t, act, 0.0), axis=-1, keepdims=True)
            ab_scr[t] = jnp.broadcast_to(a_t, ab_scr.shape[1:])

        for t in range(TT):
            base = t * PEER_PICKS
            accs = [jnp.zeros((ROW_SUBLANES, 128), F32) for _ in range(4)]
            for k in range(PEER_PICKS):
                if k % ROW_SUBLANES < ISSUE_PER_GROUP:
                    issue(1)
                gv = lax.bitcast_convert_type(cur[base + k] << 16, F32)
                accs[k % 4] = accs[k % 4] + gv * jnp.broadcast_to(ab_scr[t, k:k + 1, :], gv.shape)
            y_ref[tok0 + t] = x1_ref[tok0 + t] + ((accs[0] + accs[1]) + (accs[2] + accs[3]))
        issue(rows)

    @pl.when(i == 0)
    def _():
        def body(c, carry):
            for r in range(PEER_PROLOGUE_UNROLL):
                j = c * PEER_PROLOGUE_UNROLL + r
                start_row(ids_cur_ref, j, buf_a, sem.at[0], j, r % 2)
            return carry
        lax.fori_loop(0, rows // PEER_PROLOGUE_UNROLL, body, 0)

    process(buf_a, sem.at[0], buf_b, sem.at[1], ids_cur_ref, rows, 0)
    process(buf_b, sem.at[1], buf_a, sem.at[0], ids_nxt_ref, 0, TT)

    @pl.when(i == pl.num_programs(0) - 1)
    def _():
        wait_tile(buf_a, sem.at[0])


def _peer_ffn(ids, gt, x1_rows, g2_rows, table):
    N = x1_rows.shape[0]
    TT = PEER_TOK_TILE
    ns = N // (2 * TT)
    rows = TT * PEER_PICKS
    ids3 = ids.reshape(ns, 1, 2 * rows)
    tok = pl.BlockSpec((2 * TT, ROW_SUBLANES, 128), lambda i: (i, 0, 0))
    return pl.pallas_call(
        _peer_ffn_kernel,
        grid=(ns,),
        in_specs=[
            pl.BlockSpec((1, 1, 2 * rows), lambda i: (i, 0, 0), memory_space=pltpu.SMEM),
            pl.BlockSpec((1, 1, 2 * rows), lambda i: (jnp.minimum(i + 1, ns - 1), 0, 0),
                         memory_space=pltpu.SMEM),
            tok,
            pl.BlockSpec((ROW_SUBLANES, 128), lambda i: (0, 0)),
            pl.BlockSpec((1, PEER_PICKS, PEER_ROUTE_TILE),
                         lambda i: (i * 2 * TT // PEER_ROUTE_TILE, 0, 0)),
            pl.BlockSpec(memory_space=pl.ANY),
        ],
        out_specs=tok,
        out_shape=jax.ShapeDtypeStruct(x1_rows.shape, F32),
        scratch_shapes=[
            pltpu.VMEM((rows, ROW_SUBLANES, 128), jnp.int32),
            pltpu.VMEM((rows, ROW_SUBLANES, 128), jnp.int32),
            pltpu.VMEM((TT, PEER_PICKS, 128), F32),
            pltpu.VMEM((TT, PEER_PICKS, 128), F32),
            pltpu.SemaphoreType.DMA((2,)),
        ],
        compiler_params=pltpu.CompilerParams(
            dimension_semantics=("arbitrary",), vmem_limit_bytes=VMEM_LIMIT),
        name="peer_ffn",
    )(ids3, ids3, x1_rows, g2_rows, gt, table)


def _rope_partner(width):
    c = np.arange(width)
    j = c % ROPE_HALF
    return np.where(j < ROPE_QUARTER, c + ROPE_QUARTER, c - ROPE_QUARTER)


def _rope_tables(n):
    rows = n // GRID_W
    row = jnp.repeat(jnp.arange(rows, dtype=F32), GRID_W)
    col = jnp.tile(jnp.arange(GRID_W, dtype=F32), rows)
    inv = 1.0 / (ROPE_THETA ** (jnp.arange(0, ROPE_HALF, 2, dtype=F32) / ROPE_HALF))
    ang_r = row[:, None] * inv
    ang_c = col[:, None] * inv
    cos = jnp.concatenate([jnp.cos(ang_r)] * 2 + [jnp.cos(ang_c)] * 2, axis=-1)
    sin = jnp.concatenate([-jnp.sin(ang_r), jnp.sin(ang_r), -jnp.sin(ang_c), jnp.sin(ang_c)], axis=-1)
    return jnp.tile(cos, (1, N_Q_HEADS)), jnp.tile(sin, (1, N_Q_HEADS))


def _layer(x, p):
    B, n, D = x.shape
    N = B * n
    x2d = x.reshape(N, D)
    cos_t, sin_t = _rope_tables(n)
    yp, qt, k, vt = _in_proj(x2d, n, p["g1"], p["w_ext"], p["pw"], p["gq"], p["gqs"], p["gk"],
                             p["gks"], cos_t, sin_t, p["mavg"])
    attn_t = _attention(qt, k.reshape(B, n, KV_WIDTH), vt)
    x1, xn = _out_proj(x2d, n, yp, attn_t, p["scale"], p["w_out"], p["g2"])
    ids, gt = _peer_route(xn, p["wqt"], p["sk"])
    y = _peer_ffn(ids, gt, x1.reshape(N, ROW_SUBLANES, 128), p["g2"].reshape(ROW_SUBLANES, 128),
                  p["table"])
    return y.reshape(B, n, D)


def _prepare(norm1_g, w_in, pool_w, pool_scale, q_norm_g, k_norm_g, w_out, norm2_g,
             peer_wq, peer_subkeys, peer_u, peer_v):
    pq = _rope_partner(ATTN_WIDTH)
    pk = _rope_partner(KV_WIDTH)
    o_q = POOL_WIDTH
    o_k = o_q + ATTN_WIDTH
    w_ext = jnp.concatenate([w_in, w_in[:, o_q + pq], w_in[:, o_k + pk]], axis=1).astype(BF16)
    gq = jnp.tile(q_norm_g, N_Q_HEADS)
    gk = jnp.tile(k_norm_g, N_KV_HEADS)
    blk = np.arange(ATTN_WIDTH) // HEAD_DIM
    mavg = jnp.asarray((blk[:, None] == blk[None, :]) / HEAD_DIM, BF16)
    ub = lax.bitcast_convert_type(peer_u.astype(BF16), jnp.uint16).astype(jnp.uint32)
    vb = lax.bitcast_convert_type(peer_v.astype(BF16), jnp.uint16).astype(jnp.uint32)
    table = lax.bitcast_convert_type((ub << 16) | vb, jnp.int32).reshape(-1, ROW_SUBLANES, 128)
    return dict(
        g1=norm1_g[None, :], w_ext=w_ext, pw=pool_w.astype(BF16),
        gq=gq[None, :], gqs=gq[pq][None, :], gk=gk[None, :], gks=gk[pk][None, :], mavg=mavg,
        scale=pool_scale[None, :], w_out=w_out.astype(BF16), g2=norm2_g[None, :],
        wqt=peer_wq.T.astype(BF16),
        sk=peer_subkeys.reshape(PEER_HEADS * 2, PEER_NKEYS, PEER_HALF).astype(BF16),
        table=table,
    )


def kernel(x_prompt, x_sample, norm1_g, w_in, pool_w, pool_scale, q_norm_g, k_norm_g, w_out,
           norm2_g, peer_wq, peer_subkeys, peer_u, peer_v):
    y_prompt, y_sample = x_prompt, x_sample
    for l in range(norm1_g.shape[0]):
        p = _prepare(norm1_g[l], w_in[l], pool_w[l], pool_scale[l], q_norm_g[l], k_norm_g[l],
                     w_out[l], norm2_g[l], peer_wq[l], peer_subkeys[l], peer_u[l], peer_v[l])
        y_prompt = _layer(y_prompt, p)
        y_sample = _layer(y_sample, p)
    return (y_prompt, y_sample)
```

```python
import functools
import math

import numpy as np
import jax
import jax.numpy as jnp
from jax import lax
from jax.experimental import pallas as pl
from jax.experimental.pallas import tpu as pltpu

F32 = jnp.float32
BF16 = jnp.bfloat16

EPS = 1e-6
GRID_W = 64
POOL_WINDOWS = (2, 4, 8, 16)
POOL_GROUP_DIM = 128
POOL_WIDTH = 512
POOL_HALO = 8
HEAD_DIM = 64
N_Q_HEADS = 8
N_KV_HEADS = 2
GQA_GROUP = N_Q_HEADS // N_KV_HEADS
ATTN_WIDTH = N_Q_HEADS * HEAD_DIM
KV_WIDTH = N_KV_HEADS * HEAD_DIM
ROPE_HALF = HEAD_DIM // 2
ROPE_QUARTER = ROPE_HALF // 2
ROPE_THETA = 10000.0
ATTN_SCALE = 1.0 / math.sqrt(HEAD_DIM)
LOG2_E = math.log2(math.e)
PEER_HEADS = 8
PEER_NKEYS = 128
PEER_HALF = 128
PEER_TOPK = 16
PEER_PICKS = PEER_HEADS * PEER_TOPK

VMEM_LIMIT = 48 * 1024 * 1024


def _tile(n, want):
    t = min(n, want)
    assert n % t == 0, (n, t)
    return t


def _head_mean_square(a, mavg):
    sq = a * a
    hi = sq.astype(BF16)
    lo = (sq - hi.astype(F32)).astype(BF16)
    return (jnp.dot(hi, mavg, preferred_element_type=F32)
            + jnp.dot(lo, mavg, preferred_element_type=F32))


def _in_proj_kernel(x_ref, g1_ref, w_ref, pw_ref, gq_ref, gqs_ref, gk_ref, gks_ref,
                    cos_ref, sin_ref, mavg_ref, yp_ref, qt_ref, k_ref, vt_ref):
    x = x_ref[...]
    ms = jnp.mean(x * x, axis=-1, keepdims=True)
    h = (x * lax.rsqrt(ms + EPS) * g1_ref[...]).astype(BF16)
    z = jnp.dot(h, w_ref[...], preferred_element_type=F32)
    for g in range(len(POOL_WINDOWS)):
        sl = slice(g * POOL_GROUP_DIM, (g + 1) * POOL_GROUP_DIM)
        yp_ref[:, sl] = jnp.dot(z[:, sl].astype(BF16), pw_ref[g], preferred_element_type=F32)
    o_q = POOL_WIDTH
    o_k = o_q + ATTN_WIDTH
    o_v = o_k + KV_WIDTH
    o_qs = o_v + KV_WIDTH
    o_ks = o_qs + ATTN_WIDTH
    zq, zk, zv = z[:, o_q:o_k], z[:, o_k:o_v], z[:, o_v:o_qs]
    zqs, zks = z[:, o_qs:o_ks], z[:, o_ks:o_ks + KV_WIDTH]
    cos = cos_ref[...]
    sin = sin_ref[...]
    rq = lax.rsqrt(_head_mean_square(zq, mavg_ref[...]) + EPS)
    q = ((zq * rq * gq_ref[...]) * cos + (zqs * rq * gqs_ref[...]) * sin) * (ATTN_SCALE * LOG2_E)
    qt_ref[0] = q.T.astype(BF16)
    rk = lax.rsqrt(_head_mean_square(zk, mavg_ref[:KV_WIDTH, :KV_WIDTH]) + EPS)
    k = (zk * rk * gk_ref[...]) * cos[:, :KV_WIDTH] + (zks * rk * gks_ref[...]) * sin[:, :KV_WIDTH]
    k_ref[...] = k.astype(BF16)
    vt_ref[0] = zv.T.astype(BF16)


def _in_proj(x2d, n, g1, w_ext, pw, gq, gqs, gk, gks, cos_t, sin_t, mavg):
    N, D = x2d.shape
    T = _tile(n, 512)
    tiles_per_seq = n // T
    const = lambda *s: pl.BlockSpec(s, lambda i: (0,) * len(s))
    return pl.pallas_call(
        _in_proj_kernel,
        grid=(N // T,),
        in_specs=[
            pl.BlockSpec((T, D), lambda i: (i, 0)),
            const(1, D), const(*w_ext.shape), const(*pw.shape),
            const(1, ATTN_WIDTH), const(1, ATTN_WIDTH), const(1, KV_WIDTH), const(1, KV_WIDTH),
            pl.BlockSpec((T, ATTN_WIDTH), lambda i: (i % tiles_per_seq, 0)),
            pl.BlockSpec((T, ATTN_WIDTH), lambda i: (i % tiles_per_seq, 0)),
            const(ATTN_WIDTH, ATTN_WIDTH),
        ],
        out_specs=[
            pl.BlockSpec((T, POOL_WIDTH), lambda i: (i, 0)),
            pl.BlockSpec((1, ATTN_WIDTH, T), lambda i: (i // tiles_per_seq, 0, i % tiles_per_seq)),
            pl.BlockSpec((T, KV_WIDTH), lambda i: (i, 0)),
            pl.BlockSpec((1, KV_WIDTH, T), lambda i: (i // tiles_per_seq, 0, i % tiles_per_seq)),
        ],
        out_shape=[
            jax.ShapeDtypeStruct((N, POOL_WIDTH), F32),
            jax.ShapeDtypeStruct((N // n, ATTN_WIDTH, n), BF16),
            jax.ShapeDtypeStruct((N, KV_WIDTH), BF16),
            jax.ShapeDtypeStruct((N // n, KV_WIDTH, n), BF16),
        ],
        compiler_params=pltpu.CompilerParams(
            dimension_semantics=("arbitrary",), vmem_limit_bytes=VMEM_LIMIT),
        name="in_proj",
    )(x2d, g1, w_ext, pw, gq, gqs, gk, gks, cos_t, sin_t, mavg)


def _attn_kernel(qt_ref, k_ref, vt_ref, ot_ref, st_scr):
    def head_rows(h):
        return slice(h * HEAD_DIM, (h + 1) * HEAD_DIM)

    def scores(h):
        j = h // GQA_GROUP
        k = k_ref[0, :, j * HEAD_DIM:(j + 1) * HEAD_DIM]
        st_scr[h % 2] = jnp.dot(k, qt_ref[0, head_rows(h), :], preferred_element_type=F32)

    scores(0)
    for h in range(N_Q_HEADS):
        if h + 1 < N_Q_HEADS:
            scores(h + 1)
        j = h // GQA_GROUP
        vt = vt_ref[0, j * HEAD_DIM:(j + 1) * HEAD_DIM, :]
        st = st_scr[h % 2]
        m = jnp.max(st, axis=0, keepdims=True)
        pt = jnp.exp2(st - m)
        l = jnp.sum(pt, axis=0, keepdims=True)
        ot = jnp.dot(vt, pt.astype(BF16), preferred_element_type=F32)
        ot_ref[0, head_rows(h), :] = (ot / l).astype(BF16)


def _attention(qt, k, vt):
    B, _, n = qt.shape
    tq = _tile(n, 256)
    return pl.pallas_call(
        _attn_kernel,
        grid=(B, n // tq),
        in_specs=[
            pl.BlockSpec((1, ATTN_WIDTH, tq), lambda b, i: (b, 0, i)),
            pl.BlockSpec((1, n, KV_WIDTH), lambda b, i: (b, 0, 0)),
            pl.BlockSpec((1, KV_WIDTH, n), lambda b, i: (b, 0, 0)),
        ],
        out_specs=pl.BlockSpec((1, ATTN_WIDTH, tq), lambda b, i: (b, 0, i)),
        out_shape=jax.ShapeDtypeStruct((B, ATTN_WIDTH, n), BF16),
        scratch_shapes=[pltpu.VMEM((2, n, tq), F32)],
        compiler_params=pltpu.CompilerParams(
            dimension_semantics=("arbitrary", "arbitrary"), vmem_limit_bytes=VMEM_LIMIT),
        name="attention",
    )(qt, k, vt)


def _out_proj_kernel(n, x_ref, yp_ref, prev_ref, next_ref, at_ref, sc_ref, w_ref, g2_ref,
                     x1_ref, xn_ref):
    T = x_ref.shape[0]
    tiles_per_seq = n // T
    si = pl.program_id(0) % tiles_per_seq
    not_first = (si > 0).astype(F32)
    not_last = (si < tiles_per_seq - 1).astype(F32)
    cur = yp_ref[...]
    ext = jnp.concatenate([prev_ref[...] * not_first, cur, next_ref[...] * not_last], axis=0)
    t = si * T + lax.broadcasted_iota(jnp.int32, (T, 1), 0)
    pooled = []
    for g, w in enumerate(POOL_WINDOWS):
        sl = slice(g * POOL_GROUP_DIM, (g + 1) * POOL_GROUP_DIM)
        eg = ext[:, sl]
        acc = eg[POOL_HALO - w // 2:POOL_HALO - w // 2 + T]
        for d in range(-w // 2 + 1, w - w // 2):
            acc = acc + eg[POOL_HALO + d:POOL_HALO + d + T]
        cnt = jnp.minimum(t + (w - w // 2), n) - jnp.maximum(t - w // 2, 0)
        pooled.append(acc / cnt.astype(F32) - cur[:, sl])
    pool = (jnp.concatenate(pooled, axis=-1) * sc_ref[...]).astype(BF16)
    mixed = (jnp.dot(pool, w_ref[:POOL_WIDTH, :], preferred_element_type=F32)
             + lax.dot_general(at_ref[0], w_ref[POOL_WIDTH:, :], (((0,), (0,)), ((), ())),
                               preferred_element_type=F32))
    x1 = x_ref[...] + mixed
    x1_ref[...] = x1
    ms = jnp.mean(x1 * x1, axis=-1, keepdims=True)
    xn_ref[...] = (x1 * lax.rsqrt(ms + EPS) * g2_ref[...]).astype(BF16)


def _out_proj(x2d, n, yp, attn, scale, w_out, g2):
    N, D = x2d.shape
    T = _tile(n, 512)
    tiles_per_seq = n // T
    hb = T // POOL_HALO
    last_hb = N // POOL_HALO - 1
    const = lambda *s: pl.BlockSpec(s, lambda i: (0,) * len(s))
    return pl.pallas_call(
        functools.partial(_out_proj_kernel, n),
        grid=(N // T,),
        in_specs=[
            pl.BlockSpec((T, D), lambda i: (i, 0)),
            pl.BlockSpec((T, POOL_WIDTH), lambda i: (i, 0)),
            pl.BlockSpec((POOL_HALO, POOL_WIDTH), lambda i: (jnp.maximum(i * hb - 1, 0), 0)),
            pl.BlockSpec((POOL_HALO, POOL_WIDTH), lambda i: (jnp.minimum((i + 1) * hb, last_hb), 0)),
            pl.BlockSpec((1, ATTN_WIDTH, T), lambda i: (i // tiles_per_seq, 0, i % tiles_per_seq)),
            const(1, POOL_WIDTH), const(*w_out.shape), const(1, D),
        ],
        out_specs=[
            pl.BlockSpec((T, D), lambda i: (i, 0)),
            pl.BlockSpec((T, D), lambda i: (i, 0)),
        ],
        out_shape=[
            jax.ShapeDtypeStruct((N, D), F32),
            jax.ShapeDtypeStruct((N, D), BF16),
        ],
        compiler_params=pltpu.CompilerParams(
            dimension_semantics=("arbitrary",), vmem_limit_bytes=VMEM_LIMIT),
        name="out_proj",
    )(x2d, yp, yp, yp, attn, scale, w_out, g2)


def _extract_top(vals, payload, k):
    R = vals.shape[0]
    rows = lax.broadcasted_iota(jnp.int32, vals.shape, 0).astype(F32)
    top_v, top_p = [], []
    for _ in range(k):
        m = jnp.max(vals, axis=0, keepdims=True)
        pos = jnp.min(jnp.where(vals == m, rows, float(R)), axis=0, keepdims=True)
        sel = rows == pos
        top_v.append(m)
        if payload is None:
            top_p.append(pos)
        else:
            top_p.append(jnp.max(jnp.where(sel, payload, -1.0), axis=0, keepdims=True))
        vals = jnp.where(sel, -jnp.inf, vals)
    return jnp.concatenate(top_v, axis=0), jnp.concatenate(top_p, axis=0)


def _pair_candidates(v0, i0, v1, i1):
    K = PEER_TOPK
    sub = lax.broadcasted_iota(jnp.int32, (8, v0.shape[1]), 0)
    vals = [v0[0:1] + v1, v0[1:2] + v1[0:8]]
    eids = [i0[0:1] * PEER_NKEYS + i1, i0[1:2] * PEER_NKEYS + i1[0:8]]
    for a in range(2, 8):
        keep = sub < K // (a + 1)
        vals.append(jnp.where(keep, v0[a:a + 1] + v1[0:8], -jnp.inf))
        eids.append(i0[a:a + 1] * PEER_NKEYS + i1[0:8])
    vals.append(v0[8:16] + v1[0:1])
    eids.append(i0[8:16] * PEER_NKEYS + i1[0:1])
    return jnp.concatenate(vals, axis=0), jnp.concatenate(eids, axis=0)


def _peer_route_kernel(xn_ref, wqt_ref, sk_ref, e_ref, gt_ref, qt_scr, et_scr):
    T = xn_ref.shape[0]
    qt_scr[...] = lax.dot_general(wqt_ref[...], xn_ref[...], (((1,), (1,)), ((), ())),
                                  preferred_element_type=F32).astype(BF16)
    def route_head(h):
        sub_v, sub_i = [], []
        for p in range(2):
            hp = h * 2 + p
            qhp = qt_scr[pl.ds(pl.multiple_of(hp * PEER_HALF, PEER_HALF), PEER_HALF), :]
            s = jnp.dot(sk_ref[hp], qhp, preferred_element_type=F32)
            tv, ti = _extract_top(s, None, PEER_TOPK)
            sub_v.append(tv)
            sub_i.append(ti)
        comb, eid = _pair_candidates(sub_v[0], sub_i[0], sub_v[1], sub_i[1])
        cv, ce = _extract_top(comb, eid, PEER_TOPK)
        ex = jnp.exp(cv - cv[0:1, :])
        gate = ex / jnp.sum(ex, axis=0, keepdims=True)
        row0 = pl.multiple_of(h * PEER_TOPK, PEER_TOPK)
        et_scr[pl.ds(row0, PEER_TOPK), :] = ce.astype(F32)
        gt_ref[0, pl.ds(row0, PEER_TOPK), :] = gate

    def heads_body(c, carry):
        for r in range(ROUTE_HEADS_PER_ITER):
            route_head(c * ROUTE_HEADS_PER_ITER + r)
        return carry

    lax.fori_loop(0, PEER_HEADS // ROUTE_HEADS_PER_ITER, heads_body, 0)
    e_ref[...] = et_scr[...].T.astype(jnp.int32)


PEER_ROUTE_TILE = 128
ROUTE_HEADS_PER_ITER = 8


def _peer_route(xn, wqt, sk):
    N, D = xn.shape
    T = PEER_ROUTE_TILE
    const = lambda *s: pl.BlockSpec(s, lambda i: (0,) * len(s))
    return pl.pallas_call(
        _peer_route_kernel,
        grid=(N // T,),
        in_specs=[pl.BlockSpec((T, D), lambda i: (i, 0)), const(*wqt.shape), const(*sk.shape)],
        out_specs=[
            pl.BlockSpec((T, PEER_PICKS), lambda i: (i, 0)),
            pl.BlockSpec((1, PEER_PICKS, T), lambda i: (i, 0, 0)),
        ],
        out_shape=[
            jax.ShapeDtypeStruct((N, PEER_PICKS), jnp.int32),
            jax.ShapeDtypeStruct((N // T, PEER_PICKS, T), F32),
        ],
        scratch_shapes=[
            pltpu.VMEM((wqt.shape[0], T), BF16),
            pltpu.VMEM((PEER_PICKS, T), F32),
        ],
        compiler_params=pltpu.CompilerParams(
            dimension_semantics=("arbitrary",), vmem_limit_bytes=VMEM_LIMIT),
        name="peer_route",
    )(xn, wqt, sk)


PEER_TOK_TILE = 16
PEER_PROLOGUE_UNROLL = 16
ROW_SUBLANES = 8
ISSUE_BEFORE_WAIT = 512
ISSUE_PER_GROUP = 3
ISSUE_PER_GATHER_STEP = 8
U_HALF_MASK = -65536


def _sum_sublanes_of_8(ps, sub):
    def comb(a, b, h):
        m = (sub & h) == 0
        if 2 * h == ROW_SUBLANES:
            return jnp.where(m, a, b) + pltpu.roll(jnp.where(m, b, a), h, 0)
        return (jnp.where(m, a, pltpu.roll(b, h, 0))
                + jnp.where(m, pltpu.roll(a, ROW_SUBLANES - h, 0), b))
    l1 = [comb(ps[2 * j], ps[2 * j + 1], 1) for j in range(4)]
    l2 = [comb(l1[0], l1[1], 2), comb(l1[2], l1[3], 2)]
    return comb(l2[0], l2[1], 4)


def _peer_ffn_kernel(ids_hbm, x1_ref, g2_ref, gt_ref, tab_ref, y_ref,
                     buf_a, buf_b, hs_scr, ab_scr, ids_a, ids_b, sem, ids_sem):
    TT = PEER_TOK_TILE
    rows = TT * PEER_PICKS
    D = ROW_SUBLANES * 128
    i = pl.program_id(0)
    sub = lax.broadcasted_iota(jnp.int32, (ROW_SUBLANES, 128), 0)
    lane_ids = lax.broadcasted_iota(jnp.int32, (PEER_PICKS, PEER_ROUTE_TILE), 1)
    g2 = g2_ref[...]

    def start_row(ids_ref, dst, dst_sem, j, prio):
        pltpu.make_async_copy(tab_ref.at[ids_ref[j]], dst.at[j], dst_sem).start(priority=prio)

    def ids_copy(step, half, dst, s):
        return pltpu.make_async_copy(ids_hbm.at[step, half], dst, ids_sem.at[s])

    def wait_tile(dst, dst_sem):
        pltpu.make_async_copy(dst, dst, dst_sem).wait()

    def process(cur, cur_sem, nxt, nxt_sem, nxt_ids_ref, tok0):
        pending = iter(range(rows))

        def issue(n):
            for _ in range(n):
                k = next(pending, None)
                if k is not None:
                    start_row(nxt_ids_ref, nxt, nxt_sem, k, k % 2)

        issue(ISSUE_BEFORE_WAIT)
        wait_tile(cur, cur_sem)
        lanes = lane_ids - ((i * 2 * TT) % PEER_ROUTE_TILE + tok0)

        for t in range(TT):
            base = t * PEER_PICKS
            x1 = x1_ref[tok0 + t]
            ms = jnp.sum(x1 * x1, axis=(0, 1), keepdims=True) * (1.0 / D)
            xt = x1 * lax.rsqrt(ms + EPS) * g2
            groups = []
            for g in range(PEER_PICKS // ROW_SUBLANES):
                issue(ISSUE_PER_GROUP)
                ps = []
                for j in range(ROW_SUBLANES):
                    w = cur[base + g * ROW_SUBLANES + j]
                    ps.append(lax.bitcast_convert_type(w & jnp.int32(U_HALF_MASK), F32) * xt)
                groups.append(_sum_sublanes_of_8(ps, sub))
            hs_scr[t] = jnp.concatenate(groups, axis=0)

        hid = jnp.zeros((PEER_PICKS, PEER_ROUTE_TILE), F32)
        for t in range(TT):
            issue(ISSUE_PER_GATHER_STEP)
            hid = jnp.where(lanes == t, jnp.sum(hs_scr[t], axis=-1, keepdims=True), hid)
        act = 0.5 * hid * (1.0 + lax.erf(hid * (1.0 / math.sqrt(2.0)))) * gt_ref[0]
        for t in range(TT):
            issue(ISSUE_PER_GATHER_STEP)
            a_t = jnp.sum(jnp.where(lanes == t, act, 0.0), axis=-1, keepdims=True)
            ab_scr[t] = jnp.broadcast_to(a_t, ab_scr.shape[1:])

        for t in range(TT):
            base = t * PEER_PICKS
            accs = [jnp.zeros((ROW_SUBLANES, 128), F32) for _ in range(4)]
            for k in range(PEER_PICKS):
                if k % ROW_SUBLANES < ISSUE_PER_GROUP:
                    issue(1)
                gv = lax.bitcast_convert_type(cur[base + k] << 16, F32)
                accs[k % 4] = accs[k % 4] + gv * jnp.broadcast_to(ab_scr[t, k:k + 1, :], gv.shape)
            y_ref[tok0 + t] = x1_ref[tok0 + t] + ((accs[0] + accs[1]) + (accs[2] + accs[3]))
        issue(rows)

    last = pl.num_programs(0) - 1
    nxt_step = jnp.minimum(i + 1, last)

    @pl.when(i == 0)
    def _():
        first = ids_copy(0, 0, ids_a, 0)
        first.start()
        first.wait()

        def body(c, carry):
            for r in range(PEER_PROLOGUE_UNROLL):
                start_row(ids_a, buf_a, sem.at[0], c * PEER_PROLOGUE_UNROLL + r, r % 2)
            return carry
        lax.fori_loop(0, rows // PEER_PROLOGUE_UNROLL, body, 0)
        odd = ids_copy(0, 1, ids_b, 1)
        odd.start()
        odd.wait()

    load_a = ids_copy(nxt_step, 0, ids_a, 0)
    load_a.start()
    process(buf_a, sem.at[0], buf_b, sem.at[1], ids_b, 0)
    load_a.wait()
    load_b = ids_copy(nxt_step, 1, ids_b, 1)
    load_b.start()
    process(buf_b, sem.at[1], buf_a, sem.at[0], ids_a, TT)
    load_b.wait()

    @pl.when(i == last)
    def _():
        wait_tile(buf_a, sem.at[0])


def _peer_ffn(ids, gt, x1_rows, g2_rows, table):
    N = x1_rows.shape[0]
    TT = PEER_TOK_TILE
    ns = N // (2 * TT)
    rows = TT * PEER_PICKS
    tok = pl.BlockSpec((2 * TT, ROW_SUBLANES, 128), lambda i: (i, 0, 0))
    return pl.pallas_call(
        _peer_ffn_kernel,
        grid=(ns,),
        in_specs=[
            pl.BlockSpec(memory_space=pl.ANY),
            tok,
            pl.BlockSpec((ROW_SUBLANES, 128), lambda i: (0, 0)),
            pl.BlockSpec((1, PEER_PICKS, PEER_ROUTE_TILE),
                         lambda i: (i * 2 * TT // PEER_ROUTE_TILE, 0, 0)),
            pl.BlockSpec(memory_space=pl.ANY),
        ],
        out_specs=tok,
        out_shape=jax.ShapeDtypeStruct(x1_rows.shape, F32),
        scratch_shapes=[
            pltpu.VMEM((rows, ROW_SUBLANES, 128), jnp.int32),
            pltpu.VMEM((rows, ROW_SUBLANES, 128), jnp.int32),
            pltpu.VMEM((TT, PEER_PICKS, 128), F32),
            pltpu.VMEM((TT, PEER_PICKS, 128), F32),
            pltpu.SMEM((rows,), jnp.int32),
            pltpu.SMEM((rows,), jnp.int32),
            pltpu.SemaphoreType.DMA((2,)),
            pltpu.SemaphoreType.DMA((2,)),
        ],
        compiler_params=pltpu.CompilerParams(
            dimension_semantics=("arbitrary",), vmem_limit_bytes=VMEM_LIMIT),
        name="peer_ffn",
    )(ids.reshape(ns, 2, rows), x1_rows, g2_rows, gt, table)


def _rope_partner(width):
    c = np.arange(width)
    j = c % ROPE_HALF
    return np.where(j < ROPE_QUARTER, c + ROPE_QUARTER, c - ROPE_QUARTER)


def _rope_tables(n):
    rows = n // GRID_W
    row = jnp.repeat(jnp.arange(rows, dtype=F32), GRID_W)
    col = jnp.tile(jnp.arange(GRID_W, dtype=F32), rows)
    inv = 1.0 / (ROPE_THETA ** (jnp.arange(0, ROPE_HALF, 2, dtype=F32) / ROPE_HALF))
    ang_r = row[:, None] * inv
    ang_c = col[:, None] * inv
    cos = jnp.concatenate([jnp.cos(ang_r)] * 2 + [jnp.cos(ang_c)] * 2, axis=-1)
    sin = jnp.concatenate([-jnp.sin(ang_r), jnp.sin(ang_r), -jnp.sin(ang_c), jnp.sin(ang_c)], axis=-1)
    return jnp.tile(cos, (1, N_Q_HEADS)), jnp.tile(sin, (1, N_Q_HEADS))


def _layer(x, p):
    B, n, D = x.shape
    N = B * n
    x2d = x.reshape(N, D)
    cos_t, sin_t = _rope_tables(n)
    yp, qt, k, vt = _in_proj(x2d, n, p["g1"], p["w_ext"], p["pw"], p["gq"], p["gqs"], p["gk"],
                             p["gks"], cos_t, sin_t, p["mavg"])
    attn_t = _attention(qt, k.reshape(B, n, KV_WIDTH), vt)
    x1, xn = _out_proj(x2d, n, yp, attn_t, p["scale"], p["w_out"], p["g2"])
    ids, gt = _peer_route(xn, p["wqt"], p["sk"])
    y = _peer_ffn(ids, gt, x1.reshape(N, ROW_SUBLANES, 128), p["g2"].reshape(ROW_SUBLANES, 128),
                  p["table"])
    return y.reshape(B, n, D)


def _prepare(norm1_g, w_in, pool_w, pool_scale, q_norm_g, k_norm_g, w_out, norm2_g,
             peer_wq, peer_subkeys, peer_u, peer_v):
    pq = _rope_partner(ATTN_WIDTH)
    pk = _rope_partner(KV_WIDTH)
    o_q = POOL_WIDTH
    o_k = o_q + ATTN_WIDTH
    w_ext = jnp.concatenate([w_in, w_in[:, o_q + pq], w_in[:, o_k + pk]], axis=1).astype(BF16)
    gq = jnp.tile(q_norm_g, N_Q_HEADS)
    gk = jnp.tile(k_norm_g, N_KV_HEADS)
    blk = np.arange(ATTN_WIDTH) // HEAD_DIM
    mavg = jnp.asarray((blk[:, None] == blk[None, :]) / HEAD_DIM, BF16)
    ub = lax.bitcast_convert_type(peer_u.astype(BF16), jnp.uint16).astype(jnp.uint32)
    vb = lax.bitcast_convert_type(peer_v.astype(BF16), jnp.uint16).astype(jnp.uint32)
    table = lax.bitcast_convert_type((ub << 16) | vb, jnp.int32).reshape(-1, ROW_SUBLANES, 128)
    return dict(
        g1=norm1_g[None, :], w_ext=w_ext, pw=pool_w.astype(BF16),
        gq=gq[None, :], gqs=gq[pq][None, :], gk=gk[None, :], gks=gk[pk][None, :], mavg=mavg,
        scale=pool_scale[None, :], w_out=w_out.astype(BF16), g2=norm2_g[None, :],
        wqt=peer_wq.T.astype(BF16),
        sk=peer_subkeys.reshape(PEER_HEADS * 2, PEER_NKEYS, PEER_HALF).astype(BF16),
        table=table,
    )


def kernel(x_prompt, x_sample, norm1_g, w_in, pool_w, pool_scale, q_norm_g, k_norm_g, w_out,
           norm2_g, peer_wq, peer_subkeys, peer_u, peer_v):
    y_prompt, y_sample = x_prompt, x_sample
    for l in range(norm1_g.shape[0]):
        p = _prepare(norm1_g[l], w_in[l], pool_w[l], pool_scale[l], q_norm_g[l], k_norm_g[l],
                     w_out[l], norm2_g[l], peer_wq[l], peer_subkeys[l], peer_u[l], peer_v[l])
        y_prompt = _layer(y_prompt, p)
        y_sample = _layer(y_sample, p)
    return (y_prompt, y_sample)
```

```python
import functools
import math

import numpy as np
import jax
import jax.numpy as jnp
from jax import lax
from jax.experimental import pallas as pl
from jax.experimental.pallas import tpu as pltpu

F32 = jnp.float32
BF16 = jnp.bfloat16

EPS = 1e-6
GRID_W = 64
POOL_WINDOWS = (2, 4, 8, 16)
POOL_GROUP_DIM = 128
POOL_WIDTH = 512
POOL_HALO = 8
HEAD_DIM = 64
N_Q_HEADS = 8
N_KV_HEADS = 2
GQA_GROUP = N_Q_HEADS // N_KV_HEADS
ATTN_WIDTH = N_Q_HEADS * HEAD_DIM
KV_WIDTH = N_KV_HEADS * HEAD_DIM
ROPE_HALF = HEAD_DIM // 2
ROPE_QUARTER = ROPE_HALF // 2
ROPE_THETA = 10000.0
ATTN_SCALE = 1.0 / math.sqrt(HEAD_DIM)
LOG2_E = math.log2(math.e)
PEER_HEADS = 8
PEER_NKEYS = 128
PEER_HALF = 128
PEER_TOPK = 16
PEER_PICKS = PEER_HEADS * PEER_TOPK

VMEM_LIMIT = 48 * 1024 * 1024


def _tile(n, want):
    t = min(n, want)
    assert n % t == 0, (n, t)
    return t


def _head_mean_square(a, mavg):
    sq = a * a
    hi = sq.astype(BF16)
    lo = (sq - hi.astype(F32)).astype(BF16)
    return (jnp.dot(hi, mavg, preferred_element_type=F32)
            + jnp.dot(lo, mavg, preferred_element_type=F32))


def _in_proj_kernel(x_ref, g1_ref, w_ref, pw_ref, gq_ref, gqs_ref, gk_ref, gks_ref,
                    cos_ref, sin_ref, mavg_ref, yp_ref, qt_ref, k_ref, vt_ref):
    x = x_ref[...]
    ms = jnp.mean(x * x, axis=-1, keepdims=True)
    h = (x * lax.rsqrt(ms + EPS) * g1_ref[...]).astype(BF16)
    z = jnp.dot(h, w_ref[...], preferred_element_type=F32)
    for g in range(len(POOL_WINDOWS)):
        sl = slice(g * POOL_GROUP_DIM, (g + 1) * POOL_GROUP_DIM)
        yp_ref[:, sl] = jnp.dot(z[:, sl].astype(BF16), pw_ref[g], preferred_element_type=F32)
    o_q = POOL_WIDTH
    o_k = o_q + ATTN_WIDTH
    o_v = o_k + KV_WIDTH
    o_qs = o_v + KV_WIDTH
    o_ks = o_qs + ATTN_WIDTH
    zq, zk, zv = z[:, o_q:o_k], z[:, o_k:o_v], z[:, o_v:o_qs]
    zqs, zks = z[:, o_qs:o_ks], z[:, o_ks:o_ks + KV_WIDTH]
    cos = cos_ref[...]
    sin = sin_ref[...]
    rq = lax.rsqrt(_head_mean_square(zq, mavg_ref[...]) + EPS)
    q = ((zq * rq * gq_ref[...]) * cos + (zqs * rq * gqs_ref[...]) * sin) * (ATTN_SCALE * LOG2_E)
    qt_ref[0] = q.T.astype(BF16)
    rk = lax.rsqrt(_head_mean_square(zk, mavg_ref[:KV_WIDTH, :KV_WIDTH]) + EPS)
    k = (zk * rk * gk_ref[...]) * cos[:, :KV_WIDTH] + (zks * rk * gks_ref[...]) * sin[:, :KV_WIDTH]
    k_ref[...] = k.astype(BF16)
    vt_ref[0] = zv.T.astype(BF16)


def _in_proj(x2d, n, g1, w_ext, pw, gq, gqs, gk, gks, cos_t, sin_t, mavg):
    N, D = x2d.shape
    T = _tile(n, 512)
    tiles_per_seq = n // T
    const = lambda *s: pl.BlockSpec(s, lambda i: (0,) * len(s))
    return pl.pallas_call(
        _in_proj_kernel,
        grid=(N // T,),
        in_specs=[
            pl.BlockSpec((T, D), lambda i: (i, 0)),
            const(1, D), const(*w_ext.shape), const(*pw.shape),
            const(1, ATTN_WIDTH), const(1, ATTN_WIDTH), const(1, KV_WIDTH), const(1, KV_WIDTH),
            pl.BlockSpec((T, ATTN_WIDTH), lambda i: (i % tiles_per_seq, 0)),
            pl.BlockSpec((T, ATTN_WIDTH), lambda i: (i % tiles_per_seq, 0)),
            const(ATTN_WIDTH, ATTN_WIDTH),
        ],
        out_specs=[
            pl.BlockSpec((T, POOL_WIDTH), lambda i: (i, 0)),
            pl.BlockSpec((1, ATTN_WIDTH, T), lambda i: (i // tiles_per_seq, 0, i % tiles_per_seq)),
            pl.BlockSpec((T, KV_WIDTH), lambda i: (i, 0)),
            pl.BlockSpec((1, KV_WIDTH, T), lambda i: (i // tiles_per_seq, 0, i % tiles_per_seq)),
        ],
        out_shape=[
            jax.ShapeDtypeStruct((N, POOL_WIDTH), F32),
            jax.ShapeDtypeStruct((N // n, ATTN_WIDTH, n), BF16),
            jax.ShapeDtypeStruct((N, KV_WIDTH), BF16),
            jax.ShapeDtypeStruct((N // n, KV_WIDTH, n), BF16),
        ],
        compiler_params=pltpu.CompilerParams(
            dimension_semantics=("arbitrary",), vmem_limit_bytes=VMEM_LIMIT),
        name="in_proj",
    )(x2d, g1, w_ext, pw, gq, gqs, gk, gks, cos_t, sin_t, mavg)


def _attn_kernel(qt_ref, k_ref, vt_ref, ot_ref, st_scr):
    def head_rows(h):
        return slice(h * HEAD_DIM, (h + 1) * HEAD_DIM)

    def scores(h):
        j = h // GQA_GROUP
        k = k_ref[0, :, j * HEAD_DIM:(j + 1) * HEAD_DIM]
        st_scr[h % 2] = jnp.dot(k, qt_ref[0, head_rows(h), :], preferred_element_type=F32)

    scores(0)
    for h in range(N_Q_HEADS):
        if h + 1 < N_Q_HEADS:
            scores(h + 1)
        j = h // GQA_GROUP
        vt = vt_ref[0, j * HEAD_DIM:(j + 1) * HEAD_DIM, :]
        st = st_scr[h % 2]
        m = jnp.max(st, axis=0, keepdims=True)
        pt = jnp.exp2(st - m)
        l = jnp.sum(pt, axis=0, keepdims=True)
        ot = jnp.dot(vt, pt.astype(BF16), preferred_element_type=F32)
        ot_ref[0, head_rows(h), :] = (ot / l).astype(BF16)


def _attention(qt, k, vt):
    B, _, n = qt.shape
    tq = _tile(n, 256)
    return pl.pallas_call(
        _attn_kernel,
        grid=(B, n // tq),
        in_specs=[
            pl.BlockSpec((1, ATTN_WIDTH, tq), lambda b, i: (b, 0, i)),
            pl.BlockSpec((1, n, KV_WIDTH), lambda b, i: (b, 0, 0)),
            pl.BlockSpec((1, KV_WIDTH, n), lambda b, i: (b, 0, 0)),
        ],
        out_specs=pl.BlockSpec((1, ATTN_WIDTH, tq), lambda b, i: (b, 0, i)),
        out_shape=jax.ShapeDtypeStruct((B, ATTN_WIDTH, n), BF16),
        scratch_shapes=[pltpu.VMEM((2, n, tq), F32)],
        compiler_params=pltpu.CompilerParams(
            dimension_semantics=("arbitrary", "arbitrary"), vmem_limit_bytes=VMEM_LIMIT),
        name="attention",
    )(qt, k, vt)


def _out_proj_kernel(n, x_ref, yp_ref, prev_ref, next_ref, at_ref, sc_ref, w_ref, g2_ref,
                     x1_ref, xn_ref):
    T = x_ref.shape[0]
    tiles_per_seq = n // T
    si = pl.program_id(0) % tiles_per_seq
    not_first = (si > 0).astype(F32)
    not_last = (si < tiles_per_seq - 1).astype(F32)
    cur = yp_ref[...]
    ext = jnp.concatenate([prev_ref[...] * not_first, cur, next_ref[...] * not_last], axis=0)
    t = si * T + lax.broadcasted_iota(jnp.int32, (T, 1), 0)
    pooled = []
    for g, w in enumerate(POOL_WINDOWS):
        sl = slice(g * POOL_GROUP_DIM, (g + 1) * POOL_GROUP_DIM)
        eg = ext[:, sl]
        acc = eg[POOL_HALO - w // 2:POOL_HALO - w // 2 + T]
        for d in range(-w // 2 + 1, w - w // 2):
            acc = acc + eg[POOL_HALO + d:POOL_HALO + d + T]
        cnt = jnp.minimum(t + (w - w // 2), n) - jnp.maximum(t - w // 2, 0)
        pooled.append(acc / cnt.astype(F32) - cur[:, sl])
    pool = (jnp.concatenate(pooled, axis=-1) * sc_ref[...]).astype(BF16)
    mixed = (jnp.dot(pool, w_ref[:POOL_WIDTH, :], preferred_element_type=F32)
             + lax.dot_general(at_ref[0], w_ref[POOL_WIDTH:, :], (((0,), (0,)), ((), ())),
                               preferred_element_type=F32))
    x1 = x_ref[...] + mixed
    x1_ref[...] = x1
    ms = jnp.mean(x1 * x1, axis=-1, keepdims=True)
    xn_ref[...] = (x1 * lax.rsqrt(ms + EPS) * g2_ref[...]).astype(BF16)


def _out_proj(x2d, n, yp, attn, scale, w_out, g2):
    N, D = x2d.shape
    T = _tile(n, 512)
    tiles_per_seq = n // T
    hb = T // POOL_HALO
    last_hb = N // POOL_HALO - 1
    const = lambda *s: pl.BlockSpec(s, lambda i: (0,) * len(s))
    return pl.pallas_call(
        functools.partial(_out_proj_kernel, n),
        grid=(N // T,),
        in_specs=[
            pl.BlockSpec((T, D), lambda i: (i, 0)),
            pl.BlockSpec((T, POOL_WIDTH), lambda i: (i, 0)),
            pl.BlockSpec((POOL_HALO, POOL_WIDTH), lambda i: (jnp.maximum(i * hb - 1, 0), 0)),
            pl.BlockSpec((POOL_HALO, POOL_WIDTH), lambda i: (jnp.minimum((i + 1) * hb, last_hb), 0)),
            pl.BlockSpec((1, ATTN_WIDTH, T), lambda i: (i // tiles_per_seq, 0, i % tiles_per_seq)),
            const(1, POOL_WIDTH), const(*w_out.shape), const(1, D),
        ],
        out_specs=[
            pl.BlockSpec((T, D), lambda i: (i, 0)),
            pl.BlockSpec((T, D), lambda i: (i, 0)),
        ],
        out_shape=[
            jax.ShapeDtypeStruct((N, D), F32),
            jax.ShapeDtypeStruct((N, D), BF16),
        ],
        compiler_params=pltpu.CompilerParams(
            dimension_semantics=("arbitrary",), vmem_limit_bytes=VMEM_LIMIT),
        name="out_proj",
    )(x2d, yp, yp, yp, attn, scale, w_out, g2)


def _extract_top(vals, payload, k):
    R = vals.shape[0]
    rows = lax.broadcasted_iota(jnp.int32, vals.shape, 0).astype(F32)
    top_v, top_p = [], []
    for _ in range(k):
        m = jnp.max(vals, axis=0, keepdims=True)
        pos = jnp.min(jnp.where(vals == m, rows, float(R)), axis=0, keepdims=True)
        sel = rows == pos
        top_v.append(m)
        if payload is None:
            top_p.append(pos)
        else:
            top_p.append(jnp.max(jnp.where(sel, payload, -1.0), axis=0, keepdims=True))
        vals = jnp.where(sel, -jnp.inf, vals)
    return jnp.concatenate(top_v, axis=0), jnp.concatenate(top_p, axis=0)


def _pair_candidates(v0, i0, v1, i1):
    K = PEER_TOPK
    sub = lax.broadcasted_iota(jnp.int32, (8, v0.shape[1]), 0)
    vals = [v0[0:1] + v1, v0[1:2] + v1[0:8]]
    eids = [i0[0:1] * PEER_NKEYS + i1, i0[1:2] * PEER_NKEYS + i1[0:8]]
    for a in range(2, 8):
        keep = sub < K // (a + 1)
        vals.append(jnp.where(keep, v0[a:a + 1] + v1[0:8], -jnp.inf))
        eids.append(i0[a:a + 1] * PEER_NKEYS + i1[0:8])
    vals.append(v0[8:16] + v1[0:1])
    eids.append(i0[8:16] * PEER_NKEYS + i1[0:1])
    return jnp.concatenate(vals, axis=0), jnp.concatenate(eids, axis=0)


def _peer_route_kernel(xn_ref, wqt_ref, sk_ref, e_ref, gt_ref, qt_scr, et_scr):
    T = xn_ref.shape[0]
    qt_scr[...] = lax.dot_general(wqt_ref[...], xn_ref[...], (((1,), (1,)), ((), ())),
                                  preferred_element_type=F32).astype(BF16)
    def route_head(h):
        sub_v, sub_i = [], []
        for p in range(2):
            hp = h * 2 + p
            qhp = qt_scr[pl.ds(pl.multiple_of(hp * PEER_HALF, PEER_HALF), PEER_HALF), :]
            s = jnp.dot(sk_ref[hp], qhp, preferred_element_type=F32)
            tv, ti = _extract_top(s, None, PEER_TOPK)
            sub_v.append(tv)
            sub_i.append(ti)
        comb, eid = _pair_candidates(sub_v[0], sub_i[0], sub_v[1], sub_i[1])
        cv, ce = _extract_top(comb, eid, PEER_TOPK)
        ex = jnp.exp(cv - cv[0:1, :])
        gate = ex / jnp.sum(ex, axis=0, keepdims=True)
        row0 = pl.multiple_of(h * PEER_TOPK, PEER_TOPK)
        et_scr[pl.ds(row0, PEER_TOPK), :] = ce.astype(F32)
        gt_ref[0, pl.ds(row0, PEER_TOPK), :] = gate

    def heads_body(c, carry):
        for r in range(ROUTE_HEADS_PER_ITER):
            route_head(c * ROUTE_HEADS_PER_ITER + r)
        return carry

    lax.fori_loop(0, PEER_HEADS // ROUTE_HEADS_PER_ITER, heads_body, 0)
    e_ref[...] = et_scr[...].T.astype(jnp.int32)


PEER_ROUTE_TILE = 128
ROUTE_HEADS_PER_ITER = 8


def _peer_route(xn, wqt, sk):
    N, D = xn.shape
    T = PEER_ROUTE_TILE
    const = lambda *s: pl.BlockSpec(s, lambda i: (0,) * len(s))
    return pl.pallas_call(
        _peer_route_kernel,
        grid=(N // T,),
        in_specs=[pl.BlockSpec((T, D), lambda i: (i, 0)), const(*wqt.shape), const(*sk.shape)],
        out_specs=[
            pl.BlockSpec((T, PEER_PICKS), lambda i: (i, 0)),
            pl.BlockSpec((1, PEER_PICKS, T), lambda i: (i, 0, 0)),
        ],
        out_shape=[
            jax.ShapeDtypeStruct((N, PEER_PICKS), jnp.int32),
            jax.ShapeDtypeStruct((N // T, PEER_PICKS, T), F32),
        ],
        scratch_shapes=[
            pltpu.VMEM((wqt.shape[0], T), BF16),
            pltpu.VMEM((PEER_PICKS, T), F32),
        ],
        compiler_params=pltpu.CompilerParams(
            dimension_semantics=("arbitrary",), vmem_limit_bytes=VMEM_LIMIT),
        name="peer_route",
    )(xn, wqt, sk)


PEER_TOK_TILE = 16
PEER_PROLOGUE_UNROLL = 16
ROW_SUBLANES = 8
ISSUE_BEFORE_WAIT = 512
ISSUE_PER_GROUP = 3
ISSUE_PER_GATHER_STEP = 8
U_HALF_MASK = -65536


def _sum_sublanes_of_8(ps, sub):
    def comb(a, b, h):
        m = (sub & h) == 0
        if 2 * h == ROW_SUBLANES:
            return jnp.where(m, a, b) + pltpu.roll(jnp.where(m, b, a), h, 0)
        return (jnp.where(m, a, pltpu.roll(b, h, 0))
                + jnp.where(m, pltpu.roll(a, ROW_SUBLANES - h, 0), b))
    l1 = [comb(ps[2 * j], ps[2 * j + 1], 1) for j in range(4)]
    l2 = [comb(l1[0], l1[1], 2), comb(l1[2], l1[3], 2)]
    return comb(l2[0], l2[1], 4)


def _peer_ffn_kernel(ids_hbm, x1_ref, g2_ref, gt_ref, tab_ref, y_ref,
                     buf_a, buf_b, hs_scr, ab_scr, ids_a, ids_b, sem, ids_sem):
    TT = PEER_TOK_TILE
    rows = TT * PEER_PICKS
    D = ROW_SUBLANES * 128
    i = pl.program_id(0)
    sub = lax.broadcasted_iota(jnp.int32, (ROW_SUBLANES, 128), 0)
    lane_ids = lax.broadcasted_iota(jnp.int32, (PEER_PICKS, PEER_ROUTE_TILE), 1)
    g2 = g2_ref[...]
    x1_rows = x1_ref[...].reshape(2 * TT, ROW_SUBLANES, 128)
    y_rows = [None] * (2 * TT)

    def start_row(ids_ref, dst, dst_sem, j, prio):
        pltpu.make_async_copy(tab_ref.at[ids_ref[j]], dst.at[j], dst_sem).start(priority=prio)

    def ids_copy(step, half, dst, s):
        return pltpu.make_async_copy(ids_hbm.at[step, half], dst, ids_sem.at[s])

    def wait_tile(dst, dst_sem):
        pltpu.make_async_copy(dst, dst, dst_sem).wait()

    def process(cur, cur_sem, nxt, nxt_sem, nxt_ids_ref, tok0):
        pending = iter(range(rows))

        def issue(n):
            for _ in range(n):
                k = next(pending, None)
                if k is not None:
                    start_row(nxt_ids_ref, nxt, nxt_sem, k, k % 2)

        issue(ISSUE_BEFORE_WAIT)
        wait_tile(cur, cur_sem)
        lanes = lane_ids - ((i * 2 * TT) % PEER_ROUTE_TILE + tok0)

        for t in range(TT):
            base = t * PEER_PICKS
            x1 = x1_rows[tok0 + t]
            ms = jnp.sum(x1 * x1, axis=(0, 1), keepdims=True) * (1.0 / D)
            xt = x1 * lax.rsqrt(ms + EPS) * g2
            groups = []
            for g in range(PEER_PICKS // ROW_SUBLANES):
                issue(ISSUE_PER_GROUP)
                ps = []
                for j in range(ROW_SUBLANES):
                    w = cur[base + g * ROW_SUBLANES + j]
                    ps.append(lax.bitcast_convert_type(w & jnp.int32(U_HALF_MASK), F32) * xt)
                groups.append(_sum_sublanes_of_8(ps, sub))
            hs_scr[t] = jnp.concatenate(groups, axis=0)

        hid = jnp.zeros((PEER_PICKS, PEER_ROUTE_TILE), F32)
        for t in range(TT):
            issue(ISSUE_PER_GATHER_STEP)
            hid = jnp.where(lanes == t, jnp.sum(hs_scr[t], axis=-1, keepdims=True), hid)
        act = 0.5 * hid * (1.0 + lax.erf(hid * (1.0 / math.sqrt(2.0)))) * gt_ref[0]
        for t in range(TT):
            issue(ISSUE_PER_GATHER_STEP)
            a_t = jnp.sum(jnp.where(lanes == t, act, 0.0), axis=-1, keepdims=True)
            ab_scr[t] = jnp.broadcast_to(a_t, ab_scr.shape[1:])

        for t in range(TT):
            base = t * PEER_PICKS
            accs = [jnp.zeros((ROW_SUBLANES, 128), F32) for _ in range(4)]
            for k in range(PEER_PICKS):
                if k % ROW_SUBLANES < ISSUE_PER_GROUP:
                    issue(1)
                gv = lax.bitcast_convert_type(cur[base + k] << 16, F32)
                accs[k % 4] = accs[k % 4] + gv * jnp.broadcast_to(ab_scr[t, k:k + 1, :], gv.shape)
            y_rows[tok0 + t] = x1_rows[tok0 + t] + ((accs[0] + accs[1]) + (accs[2] + accs[3]))
        issue(rows)

    last = pl.num_programs(0) - 1
    nxt_step = jnp.minimum(i + 1, last)

    @pl.when(i == 0)
    def _():
        first = ids_copy(0, 0, ids_a, 0)
        first.start()
        first.wait()

        def body(c, carry):
            for r in range(PEER_PROLOGUE_UNROLL):
                start_row(ids_a, buf_a, sem.at[0], c * PEER_PROLOGUE_UNROLL + r, r % 2)
            return carry
        lax.fori_loop(0, rows // PEER_PROLOGUE_UNROLL, body, 0)
        odd = ids_copy(0, 1, ids_b, 1)
        odd.start()
        odd.wait()

    load_a = ids_copy(nxt_step, 0, ids_a, 0)
    load_a.start()
    process(buf_a, sem.at[0], buf_b, sem.at[1], ids_b, 0)
    load_a.wait()
    load_b = ids_copy(nxt_step, 1, ids_b, 1)
    load_b.start()
    process(buf_b, sem.at[1], buf_a, sem.at[0], ids_a, TT)
    load_b.wait()
    y_ref[...] = jnp.stack(y_rows, axis=0).reshape(2 * TT, D)

    @pl.when(i == last)
    def _():
        wait_tile(buf_a, sem.at[0])


def _peer_ffn(ids, gt, x1, g2_rows, table):
    N = x1.shape[0]
    TT = PEER_TOK_TILE
    ns = N // (2 * TT)
    rows = TT * PEER_PICKS
    tok = pl.BlockSpec((2 * TT, ROW_SUBLANES * 128), lambda i: (i, 0))
    return pl.pallas_call(
        _peer_ffn_kernel,
        grid=(ns,),
        in_specs=[
            pl.BlockSpec(memory_space=pl.ANY),
            tok,
            pl.BlockSpec((ROW_SUBLANES, 128), lambda i: (0, 0)),
            pl.BlockSpec((1, PEER_PICKS, PEER_ROUTE_TILE),
                         lambda i: (i * 2 * TT // PEER_ROUTE_TILE, 0, 0)),
            pl.BlockSpec(memory_space=pl.ANY),
        ],
        out_specs=tok,
        out_shape=jax.ShapeDtypeStruct(x1.shape, F32),
        scratch_shapes=[
            pltpu.VMEM((rows, ROW_SUBLANES, 128), jnp.int32),
            pltpu.VMEM((rows, ROW_SUBLANES, 128), jnp.int32),
            pltpu.VMEM((TT, PEER_PICKS, 128), F32),
            pltpu.VMEM((TT, PEER_PICKS, 128), F32),
            pltpu.SMEM((rows,), jnp.int32),
            pltpu.SMEM((rows,), jnp.int32),
            pltpu.SemaphoreType.DMA((2,)),
            pltpu.SemaphoreType.DMA((2,)),
        ],
        compiler_params=pltpu.CompilerParams(
            dimension_semantics=("arbitrary",), vmem_limit_bytes=VMEM_LIMIT),
        name="peer_ffn",
    )(ids.reshape(ns, 2, rows), x1, g2_rows, gt, table)


def _rope_partner(width):
    c = np.arange(width)
    j = c % ROPE_HALF
    return np.where(j < ROPE_QUARTER, c + ROPE_QUARTER, c - ROPE_QUARTER)


def _rope_tables(n):
    rows = n // GRID_W
    row = jnp.repeat(jnp.arange(rows, dtype=F32), GRID_W)
    col = jnp.tile(jnp.arange(GRID_W, dtype=F32), rows)
    inv = 1.0 / (ROPE_THETA ** (jnp.arange(0, ROPE_HALF, 2, dtype=F32) / ROPE_HALF))
    ang_r = row[:, None] * inv
    ang_c = col[:, None] * inv
    cos = jnp.concatenate([jnp.cos(ang_r)] * 2 + [jnp.cos(ang_c)] * 2, axis=-1)
    sin = jnp.concatenate([-jnp.sin(ang_r), jnp.sin(ang_r), -jnp.sin(ang_c), jnp.sin(ang_c)], axis=-1)
    return jnp.tile(cos, (1, N_Q_HEADS)), jnp.tile(sin, (1, N_Q_HEADS))


def _layer(x, p):
    B, n, D = x.shape
    N = B * n
    x2d = x.reshape(N, D)
    cos_t, sin_t = _rope_tables(n)
    yp, qt, k, vt = _in_proj(x2d, n, p["g1"], p["w_ext"], p["pw"], p["gq"], p["gqs"], p["gk"],
                             p["gks"], cos_t, sin_t, p["mavg"])
    attn_t = _attention(qt, k.reshape(B, n, KV_WIDTH), vt)
    x1, xn = _out_proj(x2d, n, yp, attn_t, p["scale"], p["w_out"], p["g2"])
    ids, gt = _peer_route(xn, p["wqt"], p["sk"])
    y = _peer_ffn(ids, gt, x1, p["g2"].reshape(ROW_SUBLANES, 128), p["table"])
    return y.reshape(B, n, D)


def _prepare(norm1_g, w_in, pool_w, pool_scale, q_norm_g, k_norm_g, w_out, norm2_g,
             peer_wq, peer_subkeys, peer_u, peer_v):
    pq = _rope_partner(ATTN_WIDTH)
    pk = _rope_partner(KV_WIDTH)
    o_q = POOL_WIDTH
    o_k = o_q + ATTN_WIDTH
    w_ext = jnp.concatenate([w_in, w_in[:, o_q + pq], w_in[:, o_k + pk]], axis=1).astype(BF16)
    gq = jnp.tile(q_norm_g, N_Q_HEADS)
    gk = jnp.tile(k_norm_g, N_KV_HEADS)
    blk = np.arange(ATTN_WIDTH) // HEAD_DIM
    mavg = jnp.asarray((blk[:, None] == blk[None, :]) / HEAD_DIM, BF16)
    ub = lax.bitcast_convert_type(peer_u.astype(BF16), jnp.uint16).astype(jnp.uint32)
    vb = lax.bitcast_convert_type(peer_v.astype(BF16), jnp.uint16).astype(jnp.uint32)
    table = lax.bitcast_convert_type((ub << 16) | vb, jnp.int32).reshape(-1, ROW_SUBLANES, 128)
    return dict(
        g1=norm1_g[None, :], w_ext=w_ext, pw=pool_w.astype(BF16),
        gq=gq[None, :], gqs=gq[pq][None, :], gk=gk[None, :], gks=gk[pk][None, :], mavg=mavg,
        scale=pool_scale[None, :], w_out=w_out.astype(BF16), g2=norm2_g[None, :],
        wqt=peer_wq.T.astype(BF16),
        sk=peer_subkeys.reshape(PEER_HEADS * 2, PEER_NKEYS, PEER_HALF).astype(BF16),
        table=table,
    )


def kernel(x_prompt, x_sample, norm1_g, w_in, pool_w, pool_scale, q_norm_g, k_norm_g, w_out,
           norm2_g, peer_wq, peer_subkeys, peer_u, peer_v):
    y_prompt, y_sample = x_prompt, x_sample
    for l in range(norm1_g.shape[0]):
        p = _prepare(norm1_g[l], w_in[l], pool_w[l], pool_scale[l], q_norm_g[l], k_norm_g[l],
                     w_out[l], norm2_g[l], peer_wq[l], peer_subkeys[l], peer_u[l], peer_v[l])
        y_prompt = _layer(y_prompt, p)
        y_sample = _layer(y_sample, p)
    return (y_prompt, y_sample)
```

```python
import functools
import math

import numpy as np
import jax
import jax.numpy as jnp
from jax import lax
from jax.experimental import pallas as pl
from jax.experimental.pallas import tpu as pltpu

F32 = jnp.float32
BF16 = jnp.bfloat16

EPS = 1e-6
GRID_W = 64
POOL_WINDOWS = (2, 4, 8, 16)
POOL_GROUP_DIM = 128
POOL_WIDTH = 512
POOL_HALO = 8
HEAD_DIM = 64
N_Q_HEADS = 8
N_KV_HEADS = 2
GQA_GROUP = N_Q_HEADS // N_KV_HEADS
ATTN_WIDTH = N_Q_HEADS * HEAD_DIM
KV_WIDTH = N_KV_HEADS * HEAD_DIM
ROPE_HALF = HEAD_DIM // 2
ROPE_QUARTER = ROPE_HALF // 2
ROPE_THETA = 10000.0
ATTN_SCALE = 1.0 / math.sqrt(HEAD_DIM)
LOG2_E = math.log2(math.e)
PEER_HEADS = 8
PEER_NKEYS = 128
PEER_HALF = 128
PEER_TOPK = 16
PEER_PICKS = PEER_HEADS * PEER_TOPK

VMEM_LIMIT = 48 * 1024 * 1024


def _tile(n, want):
    t = min(n, want)
    assert n % t == 0, (n, t)
    return t


def _head_mean_square(a, mavg):
    sq = a * a
    hi = sq.astype(BF16)
    lo = (sq - hi.astype(F32)).astype(BF16)
    return (jnp.dot(hi, mavg, preferred_element_type=F32)
            + jnp.dot(lo, mavg, preferred_element_type=F32))


def _in_proj_kernel(x_ref, g1_ref, w_ref, pw_ref, gq_ref, gqs_ref, gk_ref, gks_ref,
                    cos_ref, sin_ref, mavg_ref, yp_ref, qt_ref, k_ref, vt_ref):
    x = x_ref[...]
    ms = jnp.mean(x * x, axis=-1, keepdims=True)
    h = (x * lax.rsqrt(ms + EPS) * g1_ref[...]).astype(BF16)
    z = jnp.dot(h, w_ref[...], preferred_element_type=F32)
    for g in range(len(POOL_WINDOWS)):
        sl = slice(g * POOL_GROUP_DIM, (g + 1) * POOL_GROUP_DIM)
        yp_ref[:, sl] = jnp.dot(z[:, sl].astype(BF16), pw_ref[g], preferred_element_type=F32)
    o_q = POOL_WIDTH
    o_k = o_q + ATTN_WIDTH
    o_v = o_k + KV_WIDTH
    o_qs = o_v + KV_WIDTH
    o_ks = o_qs + ATTN_WIDTH
    zq, zk, zv = z[:, o_q:o_k], z[:, o_k:o_v], z[:, o_v:o_qs]
    zqs, zks = z[:, o_qs:o_ks], z[:, o_ks:o_ks + KV_WIDTH]
    cos = cos_ref[...]
    sin = sin_ref[...]
    rq = lax.rsqrt(_head_mean_square(zq, mavg_ref[...]) + EPS)
    q = ((zq * rq * gq_ref[...]) * cos + (zqs * rq * gqs_ref[...]) * sin) * (ATTN_SCALE * LOG2_E)
    qt_ref[0] = q.T.astype(BF16)
    rk = lax.rsqrt(_head_mean_square(zk, mavg_ref[:KV_WIDTH, :KV_WIDTH]) + EPS)
    k = (zk * rk * gk_ref[...]) * cos[:, :KV_WIDTH] + (zks * rk * gks_ref[...]) * sin[:, :KV_WIDTH]
    k_ref[...] = k.astype(BF16)
    vt_ref[0] = zv.T.astype(BF16)


def _in_proj(x2d, n, g1, w_ext, pw, gq, gqs, gk, gks, cos_t, sin_t, mavg):
    N, D = x2d.shape
    T = _tile(n, 512)
    tiles_per_seq = n // T
    const = lambda *s: pl.BlockSpec(s, lambda i: (0,) * len(s))
    return pl.pallas_call(
        _in_proj_kernel,
        grid=(N // T,),
        in_specs=[
            pl.BlockSpec((T, D), lambda i: (i, 0)),
            const(1, D), const(*w_ext.shape), const(*pw.shape),
            const(1, ATTN_WIDTH), const(1, ATTN_WIDTH), const(1, KV_WIDTH), const(1, KV_WIDTH),
            pl.BlockSpec((T, ATTN_WIDTH), lambda i: (i % tiles_per_seq, 0)),
            pl.BlockSpec((T, ATTN_WIDTH), lambda i: (i % tiles_per_seq, 0)),
            const(ATTN_WIDTH, ATTN_WIDTH),
        ],
        out_specs=[
            pl.BlockSpec((T, POOL_WIDTH), lambda i: (i, 0)),
            pl.BlockSpec((1, ATTN_WIDTH, T), lambda i: (i // tiles_per_seq, 0, i % tiles_per_seq)),
            pl.BlockSpec((T, KV_WIDTH), lambda i: (i, 0)),
            pl.BlockSpec((1, KV_WIDTH, T), lambda i: (i // tiles_per_seq, 0, i % tiles_per_seq)),
        ],
        out_shape=[
            jax.ShapeDtypeStruct((N, POOL_WIDTH), F32),
            jax.ShapeDtypeStruct((N // n, ATTN_WIDTH, n), BF16),
            jax.ShapeDtypeStruct((N, KV_WIDTH), BF16),
            jax.ShapeDtypeStruct((N // n, KV_WIDTH, n), BF16),
        ],
        compiler_params=pltpu.CompilerParams(
            dimension_semantics=("arbitrary",), vmem_limit_bytes=VMEM_LIMIT),
        name="in_proj",
    )(x2d, g1, w_ext, pw, gq, gqs, gk, gks, cos_t, sin_t, mavg)


def _attn_kernel(qt_ref, k_ref, vt_ref, ot_ref, st_scr):
    def head_rows(h):
        return slice(h * HEAD_DIM, (h + 1) * HEAD_DIM)

    def scores(h):
        j = h // GQA_GROUP
        k = k_ref[0, :, j * HEAD_DIM:(j + 1) * HEAD_DIM]
        st_scr[h % 2] = jnp.dot(k, qt_ref[0, head_rows(h), :], preferred_element_type=F32)

    scores(0)
    for h in range(N_Q_HEADS):
        if h + 1 < N_Q_HEADS:
            scores(h + 1)
        j = h // GQA_GROUP
        vt = vt_ref[0, j * HEAD_DIM:(j + 1) * HEAD_DIM, :]
        st = st_scr[h % 2]
        m = jnp.max(st, axis=0, keepdims=True)
        pt = jnp.exp2(st - m)
        l = jnp.sum(pt, axis=0, keepdims=True)
        ot = jnp.dot(vt, pt.astype(BF16), preferred_element_type=F32)
        ot_ref[0, head_rows(h), :] = (ot / l).astype(BF16)


def _attention(qt, k, vt):
    B, _, n = qt.shape
    tq = _tile(n, 256)
    return pl.pallas_call(
        _attn_kernel,
        grid=(B, n // tq),
        in_specs=[
            pl.BlockSpec((1, ATTN_WIDTH, tq), lambda b, i: (b, 0, i)),
            pl.BlockSpec((1, n, KV_WIDTH), lambda b, i: (b, 0, 0)),
            pl.BlockSpec((1, KV_WIDTH, n), lambda b, i: (b, 0, 0)),
        ],
        out_specs=pl.BlockSpec((1, ATTN_WIDTH, tq), lambda b, i: (b, 0, i)),
        out_shape=jax.ShapeDtypeStruct((B, ATTN_WIDTH, n), BF16),
        scratch_shapes=[pltpu.VMEM((2, n, tq), F32)],
        compiler_params=pltpu.CompilerParams(
            dimension_semantics=("arbitrary", "arbitrary"), vmem_limit_bytes=VMEM_LIMIT),
        name="attention",
    )(qt, k, vt)


def _out_proj_kernel(n, x_ref, yp_ref, prev_ref, next_ref, at_ref, sc_ref, w_ref, g2_ref,
                     x1_ref, xn_ref):
    T = x_ref.shape[0]
    tiles_per_seq = n // T
    si = pl.program_id(0) % tiles_per_seq
    not_first = (si > 0).astype(F32)
    not_last = (si < tiles_per_seq - 1).astype(F32)
    cur = yp_ref[...]
    ext = jnp.concatenate([prev_ref[...] * not_first, cur, next_ref[...] * not_last], axis=0)
    t = si * T + lax.broadcasted_iota(jnp.int32, (T, 1), 0)
    pooled = []
    for g, w in enumerate(POOL_WINDOWS):
        sl = slice(g * POOL_GROUP_DIM, (g + 1) * POOL_GROUP_DIM)
        eg = ext[:, sl]
        acc = eg[POOL_HALO - w // 2:POOL_HALO - w // 2 + T]
        for d in range(-w // 2 + 1, w - w // 2):
            acc = acc + eg[POOL_HALO + d:POOL_HALO + d + T]
        cnt = jnp.minimum(t + (w - w // 2), n) - jnp.maximum(t - w // 2, 0)
        pooled.append(acc / cnt.astype(F32) - cur[:, sl])
    pool = (jnp.concatenate(pooled, axis=-1) * sc_ref[...]).astype(BF16)
    mixed = (jnp.dot(pool, w_ref[:POOL_WIDTH, :], preferred_element_type=F32)
             + lax.dot_general(at_ref[0], w_ref[POOL_WIDTH:, :], (((0,), (0,)), ((), ())),
                               preferred_element_type=F32))
    x1 = x_ref[...] + mixed
    x1_ref[...] = x1
    ms = jnp.mean(x1 * x1, axis=-1, keepdims=True)
    xn_ref[...] = (x1 * lax.rsqrt(ms + EPS) * g2_ref[...]).astype(BF16)


def _out_proj(x2d, n, yp, attn, scale, w_out, g2):
    N, D = x2d.shape
    T = _tile(n, 512)
    tiles_per_seq = n // T
    hb = T // POOL_HALO
    last_hb = N // POOL_HALO - 1
    const = lambda *s: pl.BlockSpec(s, lambda i: (0,) * len(s))
    return pl.pallas_call(
        functools.partial(_out_proj_kernel, n),
        grid=(N // T,),
        in_specs=[
            pl.BlockSpec((T, D), lambda i: (i, 0)),
            pl.BlockSpec((T, POOL_WIDTH), lambda i: (i, 0)),
            pl.BlockSpec((POOL_HALO, POOL_WIDTH), lambda i: (jnp.maximum(i * hb - 1, 0), 0)),
            pl.BlockSpec((POOL_HALO, POOL_WIDTH), lambda i: (jnp.minimum((i + 1) * hb, last_hb), 0)),
            pl.BlockSpec((1, ATTN_WIDTH, T), lambda i: (i // tiles_per_seq, 0, i % tiles_per_seq)),
            const(1, POOL_WIDTH), const(*w_out.shape), const(1, D),
        ],
        out_specs=[
            pl.BlockSpec((T, D), lambda i: (i, 0)),
            pl.BlockSpec((T, D), lambda i: (i, 0)),
        ],
        out_shape=[
            jax.ShapeDtypeStruct((N, D), F32),
            jax.ShapeDtypeStruct((N, D), BF16),
        ],
        compiler_params=pltpu.CompilerParams(
            dimension_semantics=("arbitrary",), vmem_limit_bytes=VMEM_LIMIT),
        name="out_proj",
    )(x2d, yp, yp, yp, attn, scale, w_out, g2)


def _extract_top(vals, payload, k):
    R = vals.shape[0]
    rows = lax.broadcasted_iota(jnp.int32, vals.shape, 0).astype(F32)
    top_v, top_p = [], []
    for it in range(k):
        m = jnp.max(vals, axis=0, keepdims=True)
        pos = jnp.min(jnp.where(vals == m, rows, float(R)), axis=0, keepdims=True)
        sel = rows == pos
        top_v.append(m)
        if payload is None:
            top_p.append(pos)
        else:
            top_p.append(jnp.max(jnp.where(sel, payload, -1.0), axis=0, keepdims=True))
        if it + 1 < k:
            vals = jnp.where(sel, -jnp.inf, vals)
    return jnp.concatenate(top_v, axis=0), jnp.concatenate(top_p, axis=0)


def _pair_candidates(v0, i0, v1, i1):
    T = v0.shape[1]
    sub = lax.broadcasted_iota(jnp.int32, (8, T), 0)

    def rows_of(x, segments):
        out = None
        for r, kind, p in segments:
            if kind == "rep":
                piece = jnp.broadcast_to(x[p:p + 1], (8, T))
            else:
                piece = x[0:8] if (r - p) % 8 == 0 else pltpu.roll(x[0:8], (r - p) % 8, 0)
            out = piece if out is None else jnp.where(sub >= r, piece, out)
        return out

    def pairs(a_segments, b_segments, n_valid):
        val = rows_of(v0, a_segments) + rows_of(v1, b_segments)
        if n_valid < 8:
            val = jnp.where(sub < n_valid, val, -jnp.inf)
        return val, rows_of(i0, a_segments) * PEER_NKEYS + rows_of(i1, b_segments)

    tiles = [
        (v0[0:1] + v1[0:8], i0[0:1] * PEER_NKEYS + i1[0:8]),
        (v0[0:1] + v1[8:16], i0[0:1] * PEER_NKEYS + i1[8:16]),
        (v0[1:2] + v1[0:8], i0[1:2] * PEER_NKEYS + i1[0:8]),
        pairs([(0, "rep", 2), (5, "rep", 3)], [(0, "seq", 0), (5, "seq", 0)], 8),
        pairs([(0, "rep", 3), (1, "rep", 4), (4, "rep", 5), (6, "rep", 6)],
              [(0, "seq", 3), (1, "seq", 0), (4, "seq", 0), (6, "seq", 0)], 8),
        pairs([(0, "rep", 7)], [(0, "seq", 0)], 2),
        (v0[8:16] + v1[0:1], i0[8:16] * PEER_NKEYS + i1[0:1]),
    ]
    return (jnp.concatenate([t[0] for t in tiles], axis=0),
            jnp.concatenate([t[1] for t in tiles], axis=0))


def _peer_route_kernel(xn_ref, wqt_ref, sk_ref, e_ref, gt_ref, qt_scr, et_scr):
    T = xn_ref.shape[0]
    qt_scr[...] = lax.dot_general(wqt_ref[...], xn_ref[...], (((1,), (1,)), ((), ())),
                                  preferred_element_type=F32).astype(BF16)
    def route_head(h):
        sub_v, sub_i = [], []
        for p in range(2):
            hp = h * 2 + p
            qhp = qt_scr[pl.ds(pl.multiple_of(hp * PEER_HALF, PEER_HALF), PEER_HALF), :]
            s = jnp.dot(sk_ref[hp], qhp, preferred_element_type=F32)
            tv, ti = _extract_top(s, None, PEER_TOPK)
            sub_v.append(tv)
            sub_i.append(ti)
        comb, eid = _pair_candidates(sub_v[0], sub_i[0], sub_v[1], sub_i[1])
        cv, ce = _extract_top(comb, eid, PEER_TOPK)
        ex = jnp.exp(cv - cv[0:1, :])
        gate = ex / jnp.sum(ex, axis=0, keepdims=True)
        row0 = pl.multiple_of(h * PEER_TOPK, PEER_TOPK)
        et_scr[pl.ds(row0, PEER_TOPK), :] = ce.astype(F32)
        gt_ref[0, pl.ds(row0, PEER_TOPK), :] = gate

    def heads_body(c, carry):
        for r in range(ROUTE_HEADS_PER_ITER):
            route_head(c * ROUTE_HEADS_PER_ITER + r)
        return carry

    lax.fori_loop(0, PEER_HEADS // ROUTE_HEADS_PER_ITER, heads_body, 0)
    e_ref[...] = et_scr[...].T.astype(jnp.int32)


PEER_ROUTE_TILE = 128
ROUTE_HEADS_PER_ITER = 8


def _peer_route(xn, wqt, sk):
    N, D = xn.shape
    T = PEER_ROUTE_TILE
    const = lambda *s: pl.BlockSpec(s, lambda i: (0,) * len(s))
    return pl.pallas_call(
        _peer_route_kernel,
        grid=(N // T,),
        in_specs=[pl.BlockSpec((T, D), lambda i: (i, 0)), const(*wqt.shape), const(*sk.shape)],
        out_specs=[
            pl.BlockSpec((T, PEER_PICKS), lambda i: (i, 0)),
            pl.BlockSpec((1, PEER_PICKS, T), lambda i: (i, 0, 0)),
        ],
        out_shape=[
            jax.ShapeDtypeStruct((N, PEER_PICKS), jnp.int32),
            jax.ShapeDtypeStruct((N // T, PEER_PICKS, T), F32),
        ],
        scratch_shapes=[
            pltpu.VMEM((wqt.shape[0], T), BF16),
            pltpu.VMEM((PEER_PICKS, T), F32),
        ],
        compiler_params=pltpu.CompilerParams(
            dimension_semantics=("arbitrary",), vmem_limit_bytes=VMEM_LIMIT),
        name="peer_route",
    )(xn, wqt, sk)


PEER_TOK_TILE = 16
PEER_PROLOGUE_UNROLL = 16
ROW_SUBLANES = 8
ISSUE_BEFORE_WAIT = 512
ISSUE_PER_GROUP = 3
ISSUE_PER_GATHER_STEP = 8
U_HALF_MASK = -65536


def _sum_sublanes_of_8(ps, sub):
    def comb(a, b, h):
        m = (sub & h) == 0
        if 2 * h == ROW_SUBLANES:
            return jnp.where(m, a, b) + pltpu.roll(jnp.where(m, b, a), h, 0)
        return (jnp.where(m, a, pltpu.roll(b, h, 0))
                + jnp.where(m, pltpu.roll(a, ROW_SUBLANES - h, 0), b))
    l1 = [comb(ps[2 * j], ps[2 * j + 1], 1) for j in range(4)]
    l2 = [comb(l1[0], l1[1], 2), comb(l1[2], l1[3], 2)]
    return comb(l2[0], l2[1], 4)


def _peer_ffn_kernel(ids_hbm, x1_ref, g2_ref, gt_ref, tab_ref, y_ref,
                     buf_a, buf_b, hs_scr, ab_scr, ids_a, ids_b, sem, ids_sem):
    TT = PEER_TOK_TILE
    rows = TT * PEER_PICKS
    D = ROW_SUBLANES * 128
    i = pl.program_id(0)
    sub = lax.broadcasted_iota(jnp.int32, (ROW_SUBLANES, 128), 0)
    lane_ids = lax.broadcasted_iota(jnp.int32, (PEER_PICKS, PEER_ROUTE_TILE), 1)
    g2 = g2_ref[...]
    x1_rows = x1_ref[...].reshape(2 * TT, ROW_SUBLANES, 128)
    y_rows = [None] * (2 * TT)

    def start_row(ids_ref, dst, dst_sem, j, prio):
        pltpu.make_async_copy(tab_ref.at[ids_ref[j]], dst.at[j], dst_sem).start(priority=prio)

    def ids_copy(step, half, dst, s):
        return pltpu.make_async_copy(ids_hbm.at[step, half], dst, ids_sem.at[s])

    def wait_tile(dst, dst_sem):
        pltpu.make_async_copy(dst, dst, dst_sem).wait()

    def process(cur, cur_sem, nxt, nxt_sem, nxt_ids_ref, tok0):
        pending = iter(range(rows))

        def issue(n):
            for _ in range(n):
                k = next(pending, None)
                if k is not None:
                    start_row(nxt_ids_ref, nxt, nxt_sem, k, k % 2)

        issue(ISSUE_BEFORE_WAIT)
        wait_tile(cur, cur_sem)
        lanes = lane_ids - ((i * 2 * TT) % PEER_ROUTE_TILE + tok0)

        for t in range(TT):
            base = t * PEER_PICKS
            x1 = x1_rows[tok0 + t]
            ms = jnp.sum(x1 * x1, axis=(0, 1), keepdims=True) * (1.0 / D)
            xt = x1 * lax.rsqrt(ms + EPS) * g2
            groups = []
            for g in range(PEER_PICKS // ROW_SUBLANES):
                issue(ISSUE_PER_GROUP)
                ps = []
                for j in range(ROW_SUBLANES):
                    w = cur[base + g * ROW_SUBLANES + j]
                    ps.append(lax.bitcast_convert_type(w & jnp.int32(U_HALF_MASK), F32) * xt)
                groups.append(_sum_sublanes_of_8(ps, sub))
            hs_scr[t] = jnp.concatenate(groups, axis=0)

        hid = jnp.zeros((PEER_PICKS, PEER_ROUTE_TILE), F32)
        for t in range(TT):
            issue(ISSUE_PER_GATHER_STEP)
            hid = jnp.where(lanes == t, jnp.sum(hs_scr[t], axis=-1, keepdims=True), hid)
        act = 0.5 * hid * (1.0 + lax.erf(hid * (1.0 / math.sqrt(2.0)))) * gt_ref[0]
        for t in range(TT):
            issue(ISSUE_PER_GATHER_STEP)
            a_t = jnp.sum(jnp.where(lanes == t, act, 0.0), axis=-1, keepdims=True)
            ab_scr[t] = jnp.broadcast_to(a_t, ab_scr.shape[1:])

        for t in range(TT):
            base = t * PEER_PICKS
            accs = [jnp.zeros((ROW_SUBLANES, 128), F32) for _ in range(4)]
            for k in range(PEER_PICKS):
                if k % ROW_SUBLANES < ISSUE_PER_GROUP:
                    issue(1)
                gv = lax.bitcast_convert_type(cur[base + k] << 16, F32)
                accs[k % 4] = accs[k % 4] + gv * jnp.broadcast_to(ab_scr[t, k:k + 1, :], gv.shape)
            y_rows[tok0 + t] = x1_rows[tok0 + t] + ((accs[0] + accs[1]) + (accs[2] + accs[3]))
        issue(rows)

    last = pl.num_programs(0) - 1
    nxt_step = jnp.minimum(i + 1, last)

    @pl.when(i == 0)
    def _():
        first = ids_copy(0, 0, ids_a, 0)
        first.start()
        first.wait()

        def body(c, carry):
            for r in range(PEER_PROLOGUE_UNROLL):
                start_row(ids_a, buf_a, sem.at[0], c * PEER_PROLOGUE_UNROLL + r, r % 2)
            return carry
        lax.fori_loop(0, rows // PEER_PROLOGUE_UNROLL, body, 0)
        odd = ids_copy(0, 1, ids_b, 1)
        odd.start()
        odd.wait()

    load_a = ids_copy(nxt_step, 0, ids_a, 0)
    load_a.start()
    process(buf_a, sem.at[0], buf_b, sem.at[1], ids_b, 0)
    load_a.wait()
    load_b = ids_copy(nxt_step, 1, ids_b, 1)
    load_b.start()
    process(buf_b, sem.at[1], buf_a, sem.at[0], ids_a, TT)
    load_b.wait()
    y_ref[...] = jnp.stack(y_rows, axis=0).reshape(2 * TT, D)

    @pl.when(i == last)
    def _():
        wait_tile(buf_a, sem.at[0])


def _peer_ffn(ids, gt, x1, g2_rows, table):
    N = x1.shape[0]
    TT = PEER_TOK_TILE
    ns = N // (2 * TT)
    rows = TT * PEER_PICKS
    tok = pl.BlockSpec((2 * TT, ROW_SUBLANES * 128), lambda i: (i, 0))
    return pl.pallas_call(
        _peer_ffn_kernel,
        grid=(ns,),
        in_specs=[
            pl.BlockSpec(memory_space=pl.ANY),
            tok,
            pl.BlockSpec((ROW_SUBLANES, 128), lambda i: (0, 0)),
            pl.BlockSpec((1, PEER_PICKS, PEER_ROUTE_TILE),
                         lambda i: (i * 2 * TT // PEER_ROUTE_TILE, 0, 0)),
            pl.BlockSpec(memory_space=pl.ANY),
        ],
        out_specs=tok,
        out_shape=jax.ShapeDtypeStruct(x1.shape, F32),
        scratch_shapes=[
            pltpu.VMEM((rows, ROW_SUBLANES, 128), jnp.int32),
            pltpu.VMEM((rows, ROW_SUBLANES, 128), jnp.int32),
            pltpu.VMEM((TT, PEER_PICKS, 128), F32),
            pltpu.VMEM((TT, PEER_PICKS, 128), F32),
            pltpu.SMEM((rows,), jnp.int32),
            pltpu.SMEM((rows,), jnp.int32),
            pltpu.SemaphoreType.DMA((2,)),
            pltpu.SemaphoreType.DMA((2,)),
        ],
        compiler_params=pltpu.CompilerParams(
            dimension_semantics=("arbitrary",), vmem_limit_bytes=VMEM_LIMIT),
        name="peer_ffn",
    )(ids.reshape(ns, 2, rows), x1, g2_rows, gt, table)


def _rope_partner(width):
    c = np.arange(width)
    j = c % ROPE_HALF
    return np.where(j < ROPE_QUARTER, c + ROPE_QUARTER, c - ROPE_QUARTER)


def _rope_tables(n):
    rows = n // GRID_W
    row = jnp.repeat(jnp.arange(rows, dtype=F32), GRID_W)
    col = jnp.tile(jnp.arange(GRID_W, dtype=F32), rows)
    inv = 1.0 / (ROPE_THETA ** (jnp.arange(0, ROPE_HALF, 2, dtype=F32) / ROPE_HALF))
    ang_r = row[:, None] * inv
    ang_c = col[:, None] * inv
    cos = jnp.concatenate([jnp.cos(ang_r)] * 2 + [jnp.cos(ang_c)] * 2, axis=-1)
    sin = jnp.concatenate([-jnp.sin(ang_r), jnp.sin(ang_r), -jnp.sin(ang_c), jnp.sin(ang_c)], axis=-1)
    return jnp.tile(cos, (1, N_Q_HEADS)), jnp.tile(sin, (1, N_Q_HEADS))


def _layer(x, p):
    B, n, D = x.shape
    N = B * n
    x2d = x.reshape(N, D)
    cos_t, sin_t = _rope_tables(n)
    yp, qt, k, vt = _in_proj(x2d, n, p["g1"], p["w_ext"], p["pw"], p["gq"], p["gqs"], p["gk"],
                             p["gks"], cos_t, sin_t, p["mavg"])
    attn_t = _attention(qt, k.reshape(B, n, KV_WIDTH), vt)
    x1, xn = _out_proj(x2d, n, yp, attn_t, p["scale"], p["w_out"], p["g2"])
    ids, gt = _peer_route(xn, p["wqt"], p["sk"])
    y = _peer_ffn(ids, gt, x1, p["g2"].reshape(ROW_SUBLANES, 128), p["table"])
    return y.reshape(B, n, D)


def _prepare(norm1_g, w_in, pool_w, pool_scale, q_norm_g, k_norm_g, w_out, norm2_g,
             peer_wq, peer_subkeys, peer_u, peer_v):
    pq = _rope_partner(ATTN_WIDTH)
    pk = _rope_partner(KV_WIDTH)
    o_q = POOL_WIDTH
    o_k = o_q + ATTN_WIDTH
    w_ext = jnp.concatenate([w_in, w_in[:, o_q + pq], w_in[:, o_k + pk]], axis=1).astype(BF16)
    gq = jnp.tile(q_norm_g, N_Q_HEADS)
    gk = jnp.tile(k_norm_g, N_KV_HEADS)
    blk = np.arange(ATTN_WIDTH) // HEAD_DIM
    mavg = jnp.asarray((blk[:, None] == blk[None, :]) / HEAD_DIM, BF16)
    ub = lax.bitcast_convert_type(peer_u.astype(BF16), jnp.uint16).astype(jnp.uint32)
    vb = lax.bitcast_convert_type(peer_v.astype(BF16), jnp.uint16).astype(jnp.uint32)
    table = lax.bitcast_convert_type((ub << 16) | vb, jnp.int32).reshape(-1, ROW_SUBLANES, 128)
    return dict(
        g1=norm1_g[None, :], w_ext=w_ext, pw=pool_w.astype(BF16),
        gq=gq[None, :], gqs=gq[pq][None, :], gk=gk[None, :], gks=gk[pk][None, :], mavg=mavg,
        scale=pool_scale[None, :], w_out=w_out.astype(BF16), g2=norm2_g[None, :],
        wqt=peer_wq.T.astype(BF16),
        sk=peer_subkeys.reshape(PEER_HEADS * 2, PEER_NKEYS, PEER_HALF).astype(BF16),
        table=table,
    )


def kernel(x_prompt, x_sample, norm1_g, w_in, pool_w, pool_scale, q_norm_g, k_norm_g, w_out,
           norm2_g, peer_wq, peer_subkeys, peer_u, peer_v):
    y_prompt, y_sample = x_prompt, x_sample
    for l in range(norm1_g.shape[0]):
        p = _prepare(norm1_g[l], w_in[l], pool_w[l], pool_scale[l], q_norm_g[l], k_norm_g[l],
                     w_out[l], norm2_g[l], peer_wq[l], peer_subkeys[l], peer_u[l], peer_v[l])
        y_prompt = _layer(y_prompt, p)
        y_sample = _layer(y_sample, p)
    return (y_prompt, y_sample)
```
